```python
import math
import jax, jax.numpy as jnp
from jax import lax
import numpy as np

D_MODEL = 1024
BATCH = 16
SEQ = 2048
DEPTH = 2

N_EVEN = (DEPTH + 1) // 2
N_ODD = DEPTH // 2
EPS = 1e-6
RET_HEADS = 4
RET_HEAD_DIM = D_MODEL // 8
RET_WIDTH = RET_HEADS * RET_HEAD_DIM
RET_CHUNK = 128
ROPE_BASE = 10000.0
LRU_WIDTH = D_MODEL // 2
LRU_BLOCKS = 4
LRU_BLOCK_DIM = LRU_WIDTH // LRU_BLOCKS
CONV_WIDTH = 4
LRU_C = 8.0
IN_EVEN_WIDTH = 4 * RET_WIDTH + 2 * LRU_WIDTH
S5_GROUP = 16
S5_GROUPS = D_MODEL // S5_GROUP
S5_STATE = 64
S5_CHUNK = 128
DT_MIN = 0.001
DT_MAX = 0.1
D_FF = 2816

kernel_name = "hybrid_retention_rglru_s5_macaron"


def rmsnorm(x, g):
    xf = x.astype(jnp.float32)
    y = xf * lax.rsqrt(jnp.mean(xf * xf, axis=-1, keepdims=True) + EPS)
    return (y * g.astype(jnp.float32)).astype(x.dtype)


def swiglu(x, w1, w3, w2):
    return (jax.nn.silu(x @ w1) * (x @ w3)) @ w2


def rope(x):
    S, Dh = x.shape[1], x.shape[-1]
    half = Dh // 2
    inv = ROPE_BASE ** (-jnp.arange(half, dtype=jnp.float32) / half)
    ang = jnp.arange(S, dtype=jnp.float32)[:, None] * inv[None, :]
    cos = jnp.cos(ang)[None, :, None, :]
    sin = jnp.sin(ang)[None, :, None, :]
    x1, x2 = x[..., :half], x[..., half:]
    return jnp.concatenate([x1 * cos - x2 * sin, x1 * sin + x2 * cos], axis=-1)


def retention_chunkwise(q, k, v):
    B_, S, H, Dh = q.shape
    C = RET_CHUNK
    N = S // C
    q = rope(q)
    k = rope(k) * (Dh ** -0.5)
    log_gamma = jnp.log1p(-jnp.power(2.0, -5.0 - jnp.arange(H, dtype=jnp.float32)))
    pos = jnp.arange(C, dtype=jnp.float32)
    diff = pos[:, None] - pos[None, :]
    decay = jnp.where(diff >= 0, jnp.exp(log_gamma[:, None, None] * jnp.maximum(diff, 0.0)), 0.0)
    qc = q.reshape(B_, N, C, H, Dh)
    kc = k.reshape(B_, N, C, H, Dh)
    vc = v.reshape(B_, N, C, H, Dh)
    scores = jnp.einsum('bnihd,bnjhd->bhnij', qc, kc) * decay[None, :, None, :, :]
    intra = jnp.einsum('bhnij,bnjhd->bnihd', scores, vc)
    k_decay = jnp.exp(log_gamma[:, None] * (C - 1.0 - pos)[None, :])
    kv = jnp.einsum('bnjhd,hj,bnjhe->nbhde', kc, k_decay, vc)
    chunk_decay = jnp.exp(log_gamma * C)[None, :, None, None]

    def step(R, kv_n):
        return R * chunk_decay + kv_n, R

    _, R_prev = lax.scan(step, jnp.zeros((B_, H, Dh, Dh), jnp.float32), kv)
    q_decay = jnp.exp(log_gamma[:, None] * (pos + 1.0)[None, :])
    cross = jnp.einsum('bnihd,hi,nbhde->bnihe', qc, q_decay, R_prev)
    return (intra + cross).reshape(B_, S, H, Dh)


def head_layernorm(x, g):
    mu = jnp.mean(x, axis=-1, keepdims=True)
    var = jnp.mean(jnp.square(x - mu), axis=-1, keepdims=True)
    y = (x - mu) * lax.rsqrt(var + EPS)
    B_, S, H, Dh = x.shape
    return y.reshape(B_, S, H * Dh) * g.astype(jnp.float32)


def causal_depthwise_conv(x, w, b):
    K = w.shape[0]
    S = x.shape[1]
    xp = jnp.pad(x, ((0, 0), (K - 1, 0), (0, 0)))
    out = b
    for tap in range(K):
        out = out + xp[:, tap:tap + S, :] * w[tap]
    return out


def rg_lru(x, w_a, b_a, w_i, b_i, lam):
    B_, S, W = x.shape
    xb = x.reshape(B_, S, LRU_BLOCKS, LRU_BLOCK_DIM)
    r = jax.nn.sigmoid(jnp.einsum('bsgi,gio->bsgo', xb, w_a.astype(jnp.float32)).reshape(B_, S, W) + b_a.astype(jnp.float32))
    i = jax.nn.sigmoid(jnp.einsum('bsgi,gio->bsgo', xb, w_i.astype(jnp.float32)).reshape(B_, S, W) + b_i.astype(jnp.float32))
    log_a = -LRU_C * r * jax.nn.softplus(-lam.astype(jnp.float32))
    a = jnp.exp(log_a)
    mult = jnp.sqrt(-jnp.expm1(2.0 * log_a))
    bx = mult * i * x

    def comb(c1, c2):
        a1, b1 = c1
        a2, b2 = c2
        return a1 * a2, a2 * b1 + b2

    _, h = lax.associative_scan(comb, (a, bx), axis=1)
    return h


def s5_ssm(u, lam_re, lam_im, log_dt, b_re, b_im, c_re, c_im, d):
    B_, S, H = u.shape
    G, P = lam_re.shape
    L = S5_CHUNK
    N = S // L
    lam_re = lam_re.astype(jnp.float32)
    lam_im = lam_im.astype(jnp.float32)
    dt = jnp.exp(log_dt.astype(jnp.float32))[:, None]
    mag = jnp.exp(lam_re * dt)
    lbar_re = mag * jnp.cos(lam_im * dt)
    lbar_im = mag * jnp.sin(lam_im * dt)
    den = lam_re * lam_re + lam_im * lam_im
    nr = lbar_re - 1.0
    ni = lbar_im
    f_re = ((nr * lam_re + ni * lam_im) / den)[..., None]
    f_im = ((ni * lam_re - nr * lam_im) / den)[..., None]
    b_re = b_re.astype(jnp.float32)
    b_im = b_im.astype(jnp.float32)
    bbar_re = f_re * b_re - f_im * b_im
    bbar_im = f_re * b_im + f_im * b_re
    c_re = c_re.astype(jnp.float32)
    c_im = c_im.astype(jnp.float32)
    a_re = jnp.broadcast_to(lbar_re, (B_, L, G, P))
    a_im = jnp.broadcast_to(lbar_im, (B_, L, G, P))
    uc = jnp.swapaxes(u.reshape(B_, N, L, G, S5_GROUP), 0, 1)

    def comb(c1, c2):
        ar1, ai1, br1, bi1 = c1
        ar2, ai2, br2, bi2 = c2
        return (ar1 * ar2 - ai1 * ai2,
                ar1 * ai2 + ai1 * ar2,
                ar2 * br1 - ai2 * bi1 + br2,
                ar2 * bi1 + ai2 * br1 + bi2)

    def chunk_step(h0, u_n):
        h0_re, h0_im = h0
        bu_re = jnp.einsum('blgc,gpc->blgp', u_n, bbar_re)
        bu_im = jnp.einsum('blgc,gpc->blgp', u_n, bbar_im)
        p_re, p_im, hl_re, hl_im = lax.associative_scan(comb, (a_re, a_im, bu_re, bu_im), axis=1)
        h_re = hl_re + p_re * h0_re[:, None] - p_im * h0_im[:, None]
        h_im = hl_im + p_re * h0_im[:, None] + p_im * h0_re[:, None]
        y = jnp.einsum('blgp,gcp->blgc', h_re, c_re) - jnp.einsum('blgp,gcp->blgc', h_im, c_im)
        return (h_re[:, -1], h_im[:, -1]), y

    h_init = (jnp.zeros((B_, G, P), jnp.float32), jnp.zeros((B_, G, P), jnp.float32))
    _, y = lax.scan(chunk_step, h_init, uc)
    y = jnp.swapaxes(y, 0, 1).reshape(B_, S, H)
    return y + d.astype(jnp.float32) * u


def even_mixer(h, w_in, w_out, ret_norm_g, conv_w, conv_b, w_a, b_a, w_i, b_i, lam):
    B_, S, _ = h.shape
    proj = (h @ w_in).astype(jnp.float32)
    q, k, v, g_ret, x_lru, g_lru = jnp.split(
        proj, [RET_WIDTH, 2 * RET_WIDTH, 3 * RET_WIDTH, 4 * RET_WIDTH, 4 * RET_WIDTH + LRU_WIDTH], axis=-1)
    shp = (B_, S, RET_HEADS, RET_HEAD_DIM)
    ret = retention_chunkwise(q.reshape(shp), k.reshape(shp), v.reshape(shp))
    ret = head_layernorm(ret, ret_norm_g) * jax.nn.silu(g_ret)
    xc = causal_depthwise_conv(x_lru, conv_w.astype(jnp.float32), conv_b.astype(jnp.float32))
    lru = rg_lru(xc, w_a, b_a, w_i, b_i, lam) * jax.nn.gelu(g_lru)
    merged = jnp.concatenate([ret, lru], axis=-1).astype(h.dtype)
    return merged @ w_out


def odd_mixer(h, lam_re, lam_im, log_dt, b_re, b_im, c_re, c_im, d, glu_w_a, glu_w_b):
    y = s5_ssm(h.astype(jnp.float32), lam_re, lam_im, log_dt, b_re, b_im, c_re, c_im, d)
    y = jax.nn.gelu(y).astype(h.dtype)
    return (y @ glu_w_a) * jax.nn.sigmoid(y @ glu_w_b)


def setup_inputs(seed: int = 0) -> dict:
    key = jax.random.key(seed)
    ks = jax.random.split(key, 32)
    f32 = jnp.float32

    def nrm(k, shape, scale):
        return jax.random.normal(k, shape, f32) * scale

    x = nrm(ks[0], (BATCH, SEQ, D_MODEL), 1.0)
    ffn_norm_g = 1.0 + nrm(ks[1], (DEPTH, 2, D_MODEL), 0.02)
    ffn_w1 = nrm(ks[2], (DEPTH, 2, D_MODEL, D_FF), D_MODEL ** -0.5)
    ffn_w3 = nrm(ks[3], (DEPTH, 2, D_MODEL, D_FF), D_MODEL ** -0.5)
    ffn_w2 = nrm(ks[4], (DEPTH, 2, D_FF, D_MODEL), D_FF ** -0.5)
    mix_norm_g = 1.0 + nrm(ks[5], (DEPTH, D_MODEL), 0.02)
    w_in_even = nrm(ks[6], (N_EVEN, D_MODEL, IN_EVEN_WIDTH), D_MODEL ** -0.5)
    w_out_even = nrm(ks[7], (N_EVEN, RET_WIDTH + LRU_WIDTH, D_MODEL), (RET_WIDTH + LRU_WIDTH) ** -0.5)
    ret_norm_g = 1.0 + nrm(ks[8], (N_EVEN, RET_WIDTH), 0.02)
    conv_w = nrm(ks[9], (N_EVEN, CONV_WIDTH, LRU_WIDTH), CONV_WIDTH ** -0.5)
    conv_b = nrm(ks[10], (N_EVEN, LRU_WIDTH), 0.01)
    lru_w_a = nrm(ks[11], (N_EVEN, LRU_BLOCKS, LRU_BLOCK_DIM, LRU_BLOCK_DIM), LRU_BLOCK_DIM ** -0.5)
    lru_b_a = nrm(ks[12], (N_EVEN, LRU_WIDTH), 0.01)
    lru_w_i = nrm(ks[13], (N_EVEN, LRU_BLOCKS, LRU_BLOCK_DIM, LRU_BLOCK_DIM), LRU_BLOCK_DIM ** -0.5)
    lru_b_i = nrm(ks[14], (N_EVEN, LRU_WIDTH), 0.01)
    a_c = jax.random.uniform(ks[15], (N_EVEN, LRU_WIDTH), f32, 0.9, 0.999)
    s = a_c ** (1.0 / LRU_C)
    lru_lambda = jnp.log(s) - jnp.log1p(-s)
    n = jnp.arange(S5_STATE, dtype=f32)
    s5_lambda_re = -0.5 + nrm(ks[16], (N_ODD, S5_GROUPS, S5_STATE), 0.01)
    s5_lambda_im = math.pi * n + nrm(ks[17], (N_ODD, S5_GROUPS, S5_STATE), 0.01)
    s5_log_dt = jax.random.uniform(ks[18], (N_ODD, S5_GROUPS), f32, math.log(DT_MIN), math.log(DT_MAX))
    s5_b_re = nrm(ks[19], (N_ODD, S5_GROUPS, S5_STATE, S5_GROUP), (2 * S5_GROUP) ** -0.5)
    s5_b_im = nrm(ks[20], (N_ODD, S5_GROUPS, S5_STATE, S5_GROUP), (2 * S5_GROUP) ** -0.5)
    s5_c_re = nrm(ks[21], (N_ODD, S5_GROUPS, S5_GROUP, S5_STATE), (2 * S5_STATE) ** -0.5)
    s5_c_im = nrm(ks[22], (N_ODD, S5_GROUPS, S5_GROUP, S5_STATE), (2 * S5_STATE) ** -0.5)
    s5_d = nrm(ks[23], (N_ODD, D_MODEL), 1.0)
    glu_w_a = nrm(ks[24], (N_ODD, D_MODEL, D_MODEL), D_MODEL ** -0.5)
    glu_w_b = nrm(ks[25], (N_ODD, D_MODEL, D_MODEL), D_MODEL ** -0.5)
    final_norm_g = 1.0 + nrm(ks[26], (D_MODEL,), 0.02)
    return {"x": x, "ffn_norm_g": ffn_norm_g, "ffn_w1": ffn_w1, "ffn_w3": ffn_w3, "ffn_w2": ffn_w2,
            "mix_norm_g": mix_norm_g, "w_in_even": w_in_even, "w_out_even": w_out_even,
            "ret_norm_g": ret_norm_g, "conv_w": conv_w, "conv_b": conv_b,
            "lru_w_a": lru_w_a, "lru_b_a": lru_b_a, "lru_w_i": lru_w_i, "lru_b_i": lru_b_i,
            "lru_lambda": lru_lambda, "s5_lambda_re": s5_lambda_re, "s5_lambda_im": s5_lambda_im,
            "s5_log_dt": s5_log_dt, "s5_b_re": s5_b_re, "s5_b_im": s5_b_im,
            "s5_c_re": s5_c_re, "s5_c_im": s5_c_im, "s5_d": s5_d,
            "glu_w_a": glu_w_a, "glu_w_b": glu_w_b, "final_norm_g": final_norm_g}


def reference(x, ffn_norm_g, ffn_w1, ffn_w3, ffn_w2, mix_norm_g, w_in_even, w_out_even,
              ret_norm_g, conv_w, conv_b, lru_w_a, lru_b_a, lru_w_i, lru_b_i, lru_lambda,
              s5_lambda_re, s5_lambda_im, s5_log_dt, s5_b_re, s5_b_im, s5_c_re, s5_c_im, s5_d,
              glu_w_a, glu_w_b, final_norm_g):
    for layer in range(DEPTH):
        x = x + 0.5 * swiglu(rmsnorm(x, ffn_norm_g[layer, 0]), ffn_w1[layer, 0], ffn_w3[layer, 0], ffn_w2[layer, 0])
        h = rmsnorm(x, mix_norm_g[layer])
        if layer % 2 == 0:
            e = layer // 2
            x = x + even_mixer(h, w_in_even[e], w_out_even[e], ret_norm_g[e], conv_w[e], conv_b[e],
                               lru_w_a[e], lru_b_a[e], lru_w_i[e], lru_b_i[e], lru_lambda[e])
        else:
            o = layer // 2
            x = x + odd_mixer(h, s5_lambda_re[o], s5_lambda_im[o], s5_log_dt[o], s5_b_re[o], s5_b_im[o],
                              s5_c_re[o], s5_c_im[o], s5_d[o], glu_w_a[o], glu_w_b[o])
        x = x + 0.5 * swiglu(rmsnorm(x, ffn_norm_g[layer, 1]), ffn_w1[layer, 1], ffn_w3[layer, 1], ffn_w2[layer, 1])
    return rmsnorm(x, final_norm_g)
```

```python
import functools
import math

import jax
import jax.numpy as jnp
from jax import lax
from jax.experimental import pallas as pl
from jax.experimental.pallas import tpu as pltpu

F32 = jnp.float32
BF16 = jnp.bfloat16

D_MODEL = 1024
D_FF = 2816
EPS = 1e-6
RET_HEADS = 4
HEAD_DIM = 128
RET_WIDTH = RET_HEADS * HEAD_DIM
ROPE_BASE = 10000.0
LRU_WIDTH = 512
LRU_BLOCKS = 4
LRU_BLOCK_DIM = 128
CONV_WIDTH = 4
LRU_C = 8.0
IN_EVEN_WIDTH = 4 * RET_WIDTH + 2 * LRU_WIDTH
S5_GROUP = 16
S5_GROUPS = 64
S5_STATE = 64
S5_STEP = 8
S5_PAIR = 2 * S5_STEP * S5_GROUP

VMEM_LIMIT_BYTES = 56 * 1024 * 1024

FFN_ROWS = 512
FFN_COLS = 256
MIX_ROWS = 256
GLU_ROWS = 512
S5_LANES = 512


def _rms(x, g):
    return x * lax.rsqrt(jnp.mean(x * x, axis=-1, keepdims=True) + EPS) * g


def _dot(a, b):
    return jnp.dot(a, b, preferred_element_type=F32)


def _const_spec(shape):
    nd = len(shape)
    return pl.BlockSpec(shape, lambda *_: (0,) * nd, pipeline_mode=pl.Buffered(1))


def _ffn_body(post, x_ref, g_ref, w1_ref, w3_ref, w2_ref, g2_ref, *out_refs):
    x = x_ref[...]
    xn = _rms(x, g_ref[...]).astype(BF16)
    acc = jnp.zeros(x.shape, F32)
    for f in range(0, D_FF, FFN_COLS):
        h1 = _dot(xn, w1_ref[:, f:f + FFN_COLS])
        h3 = _dot(xn, w3_ref[:, f:f + FFN_COLS])
        gate = (jax.nn.silu(h1) * h3).astype(BF16)
        acc = acc + _dot(gate, w2_ref[f:f + FFN_COLS, :])
    y = x + 0.5 * acc
    if post == "final":
        out_refs[0][...] = _rms(y, g2_ref[...])
    else:
        out_refs[0][...] = y
        if post == "normed":
            out_refs[1][...] = _rms(y, g2_ref[...]).astype(BF16)


def _ffn(x, g, w1, w3, w2, g2, post):
    t = x.shape[0]
    row_spec = pl.BlockSpec((FFN_ROWS, D_MODEL), lambda i: (i, 0))
    out_shape = [jax.ShapeDtypeStruct((t, D_MODEL), F32)]
    out_specs = [row_spec]
    if post == "normed":
        out_shape.append(jax.ShapeDtypeStruct((t, D_MODEL), BF16))
        out_specs.append(row_spec)
    return pl.pallas_call(
        functools.partial(_ffn_body, post),
        grid=(t // FFN_ROWS,),
        in_specs=[row_spec, _const_spec((1, D_MODEL)), _const_spec((D_MODEL, D_FF)),
                  _const_spec((D_MODEL, D_FF)), _const_spec((D_FF, D_MODEL)), _const_spec((1, D_MODEL))],
        out_specs=out_specs,
        out_shape=out_shape,
        compiler_params=pltpu.CompilerParams(dimension_semantics=("parallel",),
                                             vmem_limit_bytes=VMEM_LIMIT_BYTES),
        name="ffn_" + post,
    )(x, g, w1, w3, w2, g2)


def _even_body(x_ref, gmix_ref, win_ref, wout_ref, rg_ref, cw_ref, cb_ref, wg_ref, ba_ref, bi_ref,
               lam_ref, cos_ref, sin_ref, dec_ref, qdec_ref, kdec_ref, cdec_ref,
               out_ref, r_ref, hc_ref, xl_ref, merged_ref):
    tc = x_ref.shape[0]

    @pl.when(pl.program_id(1) == 0)
    def _():
        r_ref[...] = jnp.zeros(r_ref.shape, F32)
        hc_ref[...] = jnp.zeros(hc_ref.shape, F32)
        xl_ref[0:8, :] = jnp.zeros((8, LRU_WIDTH), F32)

    x = x_ref[...]
    h = _rms(x, gmix_ref[...]).astype(BF16)
    proj = _dot(h, win_ref[...])

    cos = cos_ref[...]
    sin = sin_ref[...]
    for hd in range(RET_HEADS):
        lanes = slice(hd * HEAD_DIM, (hd + 1) * HEAD_DIM)
        q = proj[:, hd * HEAD_DIM:(hd + 1) * HEAD_DIM]
        k = proj[:, RET_WIDTH + hd * HEAD_DIM:RET_WIDTH + (hd + 1) * HEAD_DIM]
        v = proj[:, 2 * RET_WIDTH + hd * HEAD_DIM:2 * RET_WIDTH + (hd + 1) * HEAD_DIM]
        g_ret = proj[:, 3 * RET_WIDTH + hd * HEAD_DIM:3 * RET_WIDTH + (hd + 1) * HEAD_DIM]
        q = q * cos + pltpu.roll(q, HEAD_DIM // 2, 1) * sin
        k = (k * cos + pltpu.roll(k, HEAD_DIM // 2, 1) * sin) * (HEAD_DIM ** -0.5)
        vb = v.astype(BF16)
        scores = lax.dot_general(q.astype(BF16), k.astype(BF16), (((1,), (1,)), ((), ())),
                                 preferred_element_type=F32) * dec_ref[hd]
        state = r_ref[hd]
        ret = _dot(scores.astype(BF16), vb) + _dot((q * qdec_ref[:, lanes]).astype(BF16), state.astype(BF16))
        kv = lax.dot_general((k * kdec_ref[:, lanes]).astype(BF16), vb, (((0,), (0,)), ((), ())),
                             preferred_element_type=F32)
        r_ref[hd] = state * cdec_ref[:, lanes] + kv
        mu = jnp.mean(ret, axis=-1, keepdims=True)
        cen = ret - mu
        var = jnp.mean(cen * cen, axis=-1, keepdims=True)
        normed = cen * lax.rsqrt(var + EPS) * rg_ref[:, lanes]
        merged_ref[:, lanes] = (normed * jax.nn.silu(g_ret)).astype(BF16)

    x_lru = proj[:, 4 * RET_WIDTH:4 * RET_WIDTH + LRU_WIDTH]
    g_lru = proj[:, 4 * RET_WIDTH + LRU_WIDTH:]
    xl_ref[8:8 + tc, :] = x_lru
    xc = cb_ref[...]
    for tap in range(CONV_WIDTH):
        start = 8 - (CONV_WIDTH - 1) + tap
        xc = xc + xl_ref[start:start + tc, :] * cw_ref[tap:tap + 1, :]
    xl_ref[0:8, :] = xl_ref[tc:tc + 8, :]
    lam = lam_ref[...]
    softplus_neg = jnp.maximum(-lam, 0.0) + jnp.log1p(jnp.exp(-jnp.abs(lam)))
    rows = lax.broadcasted_iota(jnp.int32, (tc, LRU_BLOCK_DIM), 0)
    for blk in range(LRU_BLOCKS):
        lanes = slice(blk * LRU_BLOCK_DIM, (blk + 1) * LRU_BLOCK_DIM)
        xb = xc[:, lanes]
        gates = _dot(xb.astype(BF16), wg_ref[blk])
        r = jax.nn.sigmoid(gates[:, :LRU_BLOCK_DIM] + ba_ref[:, lanes])
        i = jax.nn.sigmoid(gates[:, LRU_BLOCK_DIM:] + bi_ref[:, lanes])
        log_a = -LRU_C * r * softplus_neg[:, lanes]
        a = jnp.exp(log_a)
        mult = jnp.sqrt((1.0 + a * a) * jnp.tanh(-log_a))
        b = mult * i * xb
        d = 1
        while d < tc:
            keep = rows >= d
            a_prev = jnp.where(keep, pltpu.roll(a, d, 0), 1.0)
            b_prev = jnp.where(keep, pltpu.roll(b, d, 0), 0.0)
            b = a * b_prev + b
            a = a * a_prev
            d *= 2
        hseq = a * hc_ref[:, lanes] + b
        hc_ref[:, lanes] = hseq[tc - 1:tc, :]
        lru_lanes = slice(RET_WIDTH + blk * LRU_BLOCK_DIM, RET_WIDTH + (blk + 1) * LRU_BLOCK_DIM)
        merged_ref[:, lru_lanes] = (hseq * jax.nn.gelu(g_lru[:, lanes])).astype(BF16)

    out_ref[...] = x + _dot(merged_ref[...], wout_ref[...])


def _rope_tables(seq):
    half = HEAD_DIM // 2
    inv = ROPE_BASE ** (-jnp.arange(half, dtype=F32) / half)
    ang = jnp.arange(seq, dtype=F32)[:, None] * inv[None, :]
    cos = jnp.cos(ang)
    sin = jnp.sin(ang)
    return jnp.concatenate([cos, cos], axis=-1), jnp.concatenate([-sin, sin], axis=-1)


def _decay_tables(chunk):
    log_gamma = jnp.log1p(-jnp.power(2.0, -5.0 - jnp.arange(RET_HEADS, dtype=F32)))
    pos = jnp.arange(chunk, dtype=F32)
    diff = pos[:, None] - pos[None, :]
    decay = jnp.where(diff >= 0, jnp.exp(log_gamma[:, None, None] * jnp.maximum(diff, 0.0)), 0.0)
    k_decay = jnp.exp(log_gamma[:, None] * (chunk - 1.0 - pos)[None, :])
    q_decay = jnp.exp(log_gamma[:, None] * (pos + 1.0)[None, :])
    chunk_decay = jnp.exp(log_gamma * chunk)

    def lanes(t):
        return jnp.repeat(t.T, HEAD_DIM, axis=1)

    return decay, lanes(q_decay), lanes(k_decay), lanes(chunk_decay[:, None])


def _even_mixer(x, batch, seq, g_mix, w_in, w_out, ret_g, conv_w, conv_b, w_gates, b_a, b_i, lam):
    tc = MIX_ROWS
    tiles = seq // tc
    cos, sin = _rope_tables(seq)
    decay, q_decay, k_decay, chunk_decay = _decay_tables(tc)
    row_spec = pl.BlockSpec((tc, D_MODEL), lambda b, j: (b * tiles + j, 0))
    pos_spec = pl.BlockSpec((tc, HEAD_DIM), lambda b, j: (j, 0))
    return pl.pallas_call(
        _even_body,
        grid=(batch, tiles),
        in_specs=[row_spec, _const_spec((1, D_MODEL)), _const_spec((D_MODEL, IN_EVEN_WIDTH)),
                  _const_spec((D_MODEL, D_MODEL)), _const_spec((1, RET_WIDTH)),
                  _const_spec((CONV_WIDTH, LRU_WIDTH)), _const_spec((1, LRU_WIDTH)),
                  _const_spec((LRU_BLOCKS, LRU_BLOCK_DIM, 2 * LRU_BLOCK_DIM)),
                  _const_spec((1, LRU_WIDTH)), _const_spec((1, LRU_WIDTH)), _const_spec((1, LRU_WIDTH)),
                  pos_spec, pos_spec, _const_spec((RET_HEADS, tc, tc)),
                  _const_spec((tc, RET_WIDTH)), _const_spec((tc, RET_WIDTH)), _const_spec((1, RET_WIDTH))],
        out_specs=row_spec,
        out_shape=jax.ShapeDtypeStruct(x.shape, F32),
        scratch_shapes=[pltpu.VMEM((RET_HEADS, HEAD_DIM, HEAD_DIM), F32),
                        pltpu.VMEM((1, LRU_WIDTH), F32),
                        pltpu.VMEM((tc + 8, LRU_WIDTH), F32),
                        pltpu.VMEM((tc, D_MODEL), BF16)],
        compiler_params=pltpu.CompilerParams(dimension_semantics=("parallel", "arbitrary"),
                                             vmem_limit_bytes=VMEM_LIMIT_BYTES),
        name="even_mixer",
    )(x, g_mix, w_in, w_out, ret_g, conv_w, conv_b, w_gates, b_a, b_i, lam,
      cos, sin, decay, q_decay, k_decay, chunk_decay)


def _s5_discretize(lam_re, lam_im, log_dt):
    dt = jnp.exp(log_dt)
    mag = jnp.exp(lam_re * dt)
    lbar_re = mag * jnp.cos(lam_im * dt)
    lbar_im = mag * jnp.sin(lam_im * dt)
    den = lam_re * lam_re + lam_im * lam_im
    nr = lbar_re - 1.0
    ni = lbar_im
    f_re = (nr * lam_re + ni * lam_im) / den
    f_im = (ni * lam_re - nr * lam_im) / den
    return lbar_re, lbar_im, f_re, f_im


def _s5_ops_body(lre_row_ref, lim_row_ref, ldt_row_ref, lre_col_ref, lim_col_ref, ldt_col_ref,
                 bt_re_ref, bt_im_ref, ct_re_ref, ct_im_ref,
                 a_re_ref, a_im_ref, k_ref, mo_re_ref, mo_im_ref, l8_re_ref, l8_im_ref):
    lr, li, f_re, f_im = _s5_discretize(lre_row_ref[0], lim_row_ref[0], ldt_row_ref[0])
    bt_re = bt_re_ref[0]
    bt_im = bt_im_ref[0]
    bbar_re = f_re * bt_re - f_im * bt_im
    bbar_im = f_re * bt_im + f_im * bt_re
    ct_re = ct_re_ref[0]
    ct_im = ct_im_ref[0]
    pr = jnp.ones_like(lr)
    pi = jnp.zeros_like(li)
    for tau in range(S5_STEP):
        a_re = bbar_re * pr - bbar_im * pi
        a_im = bbar_re * pi + bbar_im * pr
        a_re_ref[0, tau] = a_re
        a_im_ref[0, tau] = a_im
        k_ref[0, tau] = (jnp.dot(a_re, ct_re, precision=lax.Precision.HIGHEST, preferred_element_type=F32)
                         - jnp.dot(a_im, ct_im, precision=lax.Precision.HIGHEST, preferred_element_type=F32))
        pr, pi = pr * lr - pi * li, pr * li + pi * lr
    l8_re_ref[0] = pr
    l8_im_ref[0] = pi

    lr_c, li_c, _, _ = _s5_discretize(lre_col_ref[0], lim_col_ref[0], ldt_col_ref[0])
    qr = lr_c
    qi = li_c
    for t in range(S5_STEP):
        mo_re_ref[0, t] = ct_re * qr - ct_im * qi
        mo_im_ref[0, t] = -(ct_re * qi) - ct_im * qr
        qr, qi = qr * lr_c - qi * li_c, qr * li_c + qi * lr_c


def _s5_operators(lam_re, lam_im, log_dt, b_re, b_im, c_re, c_im):
    g, p, c = S5_GROUPS, S5_STATE, S5_GROUP
    ldt = jnp.broadcast_to(log_dt[:, None], (g, p))
    rows = [a.reshape(g, 1, p) for a in (lam_re, lam_im, ldt)]
    cols = [a.reshape(g, p, 1) for a in (lam_re, lam_im, ldt)]
    bt = [jnp.swapaxes(a, 1, 2) for a in (b_re, b_im)]
    ct = [jnp.swapaxes(a, 1, 2) for a in (c_re, c_im)]

    def spec(*shape):
        nd = len(shape)
        return pl.BlockSpec((1,) + shape, lambda i: (i,) + (0,) * nd)

    a_re, a_im, kt, mo_re, mo_im, l8_re, l8_im = pl.pallas_call(
        _s5_ops_body,
        grid=(g,),
        in_specs=[spec(1, p)] * 3 + [spec(p, 1)] * 3 + [spec(c, p)] * 2 + [spec(p, c)] * 2,
        out_specs=[spec(S5_STEP, c, p), spec(S5_STEP, c, p), spec(S5_STEP, c, c),
                   spec(S5_STEP, p, c), spec(S5_STEP, p, c), spec(1, p), spec(1, p)],
        out_shape=[jax.ShapeDtypeStruct((g, S5_STEP, c, p), F32), jax.ShapeDtypeStruct((g, S5_STEP, c, p), F32),
                   jax.ShapeDtypeStruct((g, S5_STEP, c, c), F32),
                   jax.ShapeDtypeStruct((g, S5_STEP, p, c), F32), jax.ShapeDtypeStruct((g, S5_STEP, p, c), F32),
                   jax.ShapeDtypeStruct((g, 1, p), F32), jax.ShapeDtypeStruct((g, 1, p), F32)],
        compiler_params=pltpu.CompilerParams(dimension_semantics=("parallel",)),
        name="s5_operators",
    )(*rows, *cols, *bt, *ct)

    n = S5_STEP * c
    m_in = jnp.concatenate([a_re[:, ::-1], a_im[:, ::-1]], axis=-1).reshape(g, n, 2 * p)
    lag = jnp.arange(S5_STEP)[None, :] - jnp.arange(S5_STEP)[:, None]
    blocks = jnp.where((lag >= 0)[None, :, :, None, None], kt[:, jnp.maximum(lag, 0)], 0.0)
    m_intra = blocks.transpose(0, 1, 3, 2, 4).reshape(g, n, n)
    m_out = jnp.concatenate([mo_re.transpose(0, 2, 1, 3).reshape(g, p, n),
                             mo_im.transpose(0, 2, 1, 3).reshape(g, p, n)], axis=1)

    def pair(m, split_rows, split_cols):
        m = m.reshape(g // 2, 2, n, n)
        z = jnp.zeros_like(m[:, 0])
        full = jnp.concatenate([jnp.concatenate([m[:, 0], z], axis=2),
                                jnp.concatenate([z, m[:, 1]], axis=2)], axis=1)
        order = jnp.concatenate([jnp.arange(p), 2 * p + jnp.arange(p), p + jnp.arange(p), 3 * p + jnp.arange(p)])
        if split_rows:
            full = full[:, order, :]
        if split_cols:
            full = full[:, :, order]
        return full.astype(BF16)

    l8 = jnp.concatenate([l8_re.reshape(g // 2, 1, 2 * p), l8_im.reshape(g // 2, 1, 2 * p)], axis=-1)
    return pair(m_in, False, True), pair(m_intra, False, False), pair(m_out, True, False), l8


def _s5_body(batch, u_ref, min_ref, mintra_ref, mout_ref, l8_ref, y_ref, s_ref, h_ref):
    rows = u_ref.shape[0]
    half = S5_PAIR // 2
    pairs = S5_LANES // S5_PAIR
    for pp in range(pairs):
        lanes = slice(pp * S5_PAIR, (pp + 1) * S5_PAIR)
        s_ref[:, lanes] = _dot(u_ref[:, lanes], min_ref[pp])

    l8 = [(l8_ref[pp, :, :half], l8_ref[pp, :, half:]) for pp in range(pairs)]

    def step(n, carry):
        r0 = pl.multiple_of(n * batch, batch)
        new = []
        for pp in range(pairs):
            st_re, st_im = carry[pp]
            re_lanes = slice(pp * S5_PAIR, pp * S5_PAIR + half)
            im_lanes = slice(pp * S5_PAIR + half, (pp + 1) * S5_PAIR)
            h_ref[pl.ds(r0, batch), re_lanes] = st_re.astype(BF16)
            h_ref[pl.ds(r0, batch), im_lanes] = st_im.astype(BF16)
            lr, li = l8[pp]
            new.append((lr * st_re - li * st_im + s_ref[pl.ds(r0, batch), re_lanes],
                        lr * st_im + li * st_re + s_ref[pl.ds(r0, batch), im_lanes]))
        return tuple(new)

    zero = jnp.zeros((batch, half), F32)
    lax.fori_loop(0, rows // batch, step, tuple((zero, zero) for _ in range(pairs)))

    for pp in range(pairs):
        lanes = slice(pp * S5_PAIR, (pp + 1) * S5_PAIR)
        y_ref[:, lanes] = _dot(u_ref[:, lanes], mintra_ref[pp]) + _dot(h_ref[:, lanes], mout_ref[pp])


def _s5_scan(u, batch, m_in, m_intra, m_out, l8):
    rows, width = u.shape
    pairs = S5_LANES // S5_PAIR
    lane_spec = pl.BlockSpec((rows, S5_LANES), lambda i: (0, i))
    op_spec = pl.BlockSpec((pairs, S5_PAIR, S5_PAIR), lambda i: (i, 0, 0))
    return pl.pallas_call(
        functools.partial(_s5_body, batch),
        grid=(width // S5_LANES,),
        in_specs=[lane_spec, op_spec, op_spec, op_spec, pl.BlockSpec((pairs, 1, S5_PAIR), lambda i: (i, 0, 0))],
        out_specs=lane_spec,
        out_shape=jax.ShapeDtypeStruct((rows, width), F32),
        scratch_shapes=[pltpu.VMEM((rows, S5_LANES), F32), pltpu.VMEM((rows, S5_LANES), BF16)],
        compiler_params=pltpu.CompilerParams(dimension_semantics=("parallel",),
                                             vmem_limit_bytes=VMEM_LIMIT_BYTES),
        name="s5_scan",
    )(u, m_in, m_intra, m_out, l8)


def _glu_body(x_ref, ys_ref, gmix_ref, d_ref, wab_ref, out_ref):
    x = x_ref[...]
    u = _rms(x, gmix_ref[...])
    y = ys_ref[...] + d_ref[...] * u
    z = jax.nn.gelu(y).astype(BF16)
    ab = _dot(z, wab_ref[...])
    out_ref[...] = x + ab[:, :D_MODEL] * jax.nn.sigmoid(ab[:, D_MODEL:])


def _glu(x, ys, g_mix, d, w_ab):
    t = x.shape[0]
    row_spec = pl.BlockSpec((GLU_ROWS, D_MODEL), lambda i: (i, 0))
    return pl.pallas_call(
        _glu_body,
        grid=(t // GLU_ROWS,),
        in_specs=[row_spec, row_spec, _const_spec((1, D_MODEL)), _const_spec((1, D_MODEL)),
                  _const_spec((D_MODEL, 2 * D_MODEL))],
        out_specs=row_spec,
        out_shape=jax.ShapeDtypeStruct(x.shape, F32),
        compiler_params=pltpu.CompilerParams(dimension_semantics=("parallel",),
                                             vmem_limit_bytes=VMEM_LIMIT_BYTES),
        name="glu",
    )(x, ys, g_mix, d, w_ab)


def kernel(x, ffn_norm_g, ffn_w1, ffn_w3, ffn_w2, mix_norm_g, w_in_even, w_out_even, ret_norm_g, conv_w, conv_b, lru_w_a, lru_b_a, lru_w_i, lru_b_i, lru_lambda, s5_lambda_re, s5_lambda_im, s5_log_dt, s5_b_re, s5_b_im, s5_c_re, s5_c_im, s5_d, glu_w_a, glu_w_b, final_norm_g):
    batch, seq, _ = x.shape
    assert seq % MIX_ROWS == 0 and seq % S5_STEP == 0 and (batch * seq) % FFN_ROWS == 0
    t = batch * seq
    row = lambda v: v.reshape(1, -1).astype(F32)
    xf = x.reshape(t, D_MODEL)

    def ffn(xv, layer, half, g2, post):
        return _ffn(xv, row(ffn_norm_g[layer, half]), ffn_w1[layer, half].astype(BF16),
                    ffn_w3[layer, half].astype(BF16), ffn_w2[layer, half].astype(BF16), row(g2), post)

    (xf,) = ffn(xf, 0, 0, final_norm_g, "plain")
    w_gates = jnp.concatenate([lru_w_a[0], lru_w_i[0]], axis=-1).astype(BF16)
    xf = _even_mixer(xf, batch, seq, row(mix_norm_g[0]), w_in_even[0].astype(BF16), w_out_even[0].astype(BF16),
                     row(ret_norm_g[0]), conv_w[0].astype(F32), row(conv_b[0]), w_gates,
                     row(lru_b_a[0]), row(lru_b_i[0]), row(lru_lambda[0]))
    (xf,) = ffn(xf, 0, 1, final_norm_g, "plain")

    xf, hn = ffn(xf, 1, 0, mix_norm_g[1], "normed")
    chunks = seq // S5_STEP
    u = hn.reshape(batch, chunks, S5_STEP, S5_GROUPS, S5_GROUP).transpose(1, 0, 3, 2, 4)
    u = u.reshape(chunks * batch, S5_GROUPS * S5_STEP * S5_GROUP)
    m_in, m_intra, m_out, l8 = _s5_operators(s5_lambda_re[0], s5_lambda_im[0], s5_log_dt[0],
                                             s5_b_re[0], s5_b_im[0], s5_c_re[0], s5_c_im[0])
    ys = _s5_scan(u, batch, m_in, m_intra, m_out, l8)
    ys = ys.reshape(chunks, batch, S5_GROUPS, S5_STEP, S5_GROUP).transpose(1, 0, 3, 2, 4).reshape(t, D_MODEL)
    w_ab = jnp.concatenate([glu_w_a[0], glu_w_b[0]], axis=-1).astype(BF16)
    xf = _glu(xf, ys, row(mix_norm_g[1]), row(s5_d[0]), w_ab)
    (out,) = ffn(xf, 1, 1, final_norm_g, "final")
    return out.reshape(batch, seq, D_MODEL)
```

```python
import functools
import math

import jax
import jax.numpy as jnp
from jax import lax
from jax.experimental import pallas as pl
from jax.experimental.pallas import tpu as pltpu

F32 = jnp.float32
BF16 = jnp.bfloat16

D_MODEL = 1024
D_FF = 2816
EPS = 1e-6
RET_HEADS = 4
HEAD_DIM = 128
RET_WIDTH = RET_HEADS * HEAD_DIM
ROPE_BASE = 10000.0
LRU_WIDTH = 512
LRU_BLOCKS = 4
LRU_BLOCK_DIM = 128
CONV_WIDTH = 4
LRU_C = 8.0
IN_EVEN_WIDTH = 4 * RET_WIDTH + 2 * LRU_WIDTH
S5_GROUP = 16
S5_GROUPS = 64
S5_STATE = 64
S5_STEP = 8
S5_PAIR = 2 * S5_STEP * S5_GROUP

VMEM_LIMIT_BYTES = 56 * 1024 * 1024

FFN_ROWS = 512
FFN_COLS = 256
MIX_ROWS = 256
GLU_ROWS = 512
S5_LANES = 512
S5_OPS_GROUPS = 8


def _rms(x, g):
    return x * lax.rsqrt(jnp.mean(x * x, axis=-1, keepdims=True) + EPS) * g


def _dot(a, b):
    return jnp.dot(a, b, preferred_element_type=F32)


def _const_spec(shape):
    nd = len(shape)
    return pl.BlockSpec(shape, lambda *_: (0,) * nd, pipeline_mode=pl.Buffered(1))


def _block_transpose(v):
    lane_block = lax.broadcasted_iota(jnp.int32, v[0].shape, 1) // S5_GROUP
    for k in (4, 2, 1):
        upper = (lane_block & k) != 0
        new = list(v)
        for a in range(8):
            if a & k == 0:
                new[a] = jnp.where(upper, pltpu.roll(v[a + k], S5_GROUP * k, 1), v[a])
                new[a + k] = jnp.where(upper, v[a + k], pltpu.roll(v[a], 128 - S5_GROUP * k, 1))
        v = new
    return v


def _ffn_body(post, x_ref, g_ref, w1_ref, w3_ref, w2_ref, g2_ref, *out_refs):
    x = x_ref[...]
    xn = _rms(x, g_ref[...]).astype(BF16)
    acc = jnp.zeros(x.shape, F32)
    for f in range(0, D_FF, FFN_COLS):
        h1 = _dot(xn, w1_ref[:, f:f + FFN_COLS])
        h3 = _dot(xn, w3_ref[:, f:f + FFN_COLS])
        gate = (jax.nn.silu(h1) * h3).astype(BF16)
        acc = acc + _dot(gate, w2_ref[f:f + FFN_COLS, :])
    y = x + 0.5 * acc
    if post == "final":
        out_refs[0][...] = _rms(y, g2_ref[...])
    else:
        out_refs[0][...] = y
        if post == "chunked":
            u_ref, scr_ref = out_refs[1], out_refs[2]
            chunks = x.shape[0] // S5_STEP
            hn = _rms(y, g2_ref[...])
            for lb in range(D_MODEL // 128):
                scr_ref[lb] = hn[:, lb * 128:(lb + 1) * 128]
                steps = [scr_ref[lb, pl.ds(s, chunks, stride=S5_STEP), :]
                         for s in range(S5_STEP)]
                for g8, blk in enumerate(_block_transpose(steps)):
                    lane0 = (lb * 8 + g8) * 128
                    u_ref[:, lane0:lane0 + 128] = blk.astype(BF16)


def _ffn(x, g, w1, w3, w2, g2, post, seq):
    t = x.shape[0]
    row_spec = pl.BlockSpec((FFN_ROWS, D_MODEL), lambda i: (i, 0))
    out_shape = [jax.ShapeDtypeStruct((t, D_MODEL), F32)]
    out_specs = [row_spec]
    scratch = []
    if post == "chunked":
        tiles = seq // FFN_ROWS
        out_shape.append(jax.ShapeDtypeStruct((seq // S5_STEP, (t // seq) * S5_STEP * D_MODEL), BF16))
        out_specs.append(pl.BlockSpec((FFN_ROWS // S5_STEP, S5_STEP * D_MODEL), lambda i: (i % tiles, i // tiles)))
        scratch.append(pltpu.VMEM((D_MODEL // 128, FFN_ROWS, 128), F32))
    return pl.pallas_call(
        functools.partial(_ffn_body, post),
        grid=(t // FFN_ROWS,),
        in_specs=[row_spec, _const_spec((1, D_MODEL)), _const_spec((D_MODEL, D_FF)),
                  _const_spec((D_MODEL, D_FF)), _const_spec((D_FF, D_MODEL)), _const_spec((1, D_MODEL))],
        out_specs=out_specs,
        out_shape=out_shape,
        scratch_shapes=scratch,
        compiler_params=pltpu.CompilerParams(dimension_semantics=("parallel",),
                                             vmem_limit_bytes=VMEM_LIMIT_BYTES),
        name="ffn_" + post,
    )(x, g, w1, w3, w2, g2)


def _even_body(x_ref, gmix_ref, win_ref, wout_ref, rg_ref, cw_ref, cb_ref, wg_ref, ba_ref, bi_ref,
               lam_ref, cos_ref, sin_ref, dec_ref, qdec_ref, kdec_ref, cdec_ref,
               out_ref, r_ref, hc_ref, xl_ref, merged_ref):
    tc = x_ref.shape[0]

    @pl.when(pl.program_id(1) == 0)
    def _():
        r_ref[...] = jnp.zeros(r_ref.shape, F32)
        hc_ref[...] = jnp.zeros(hc_ref.shape, F32)
        xl_ref[0:8, :] = jnp.zeros((8, LRU_WIDTH), F32)

    x = x_ref[...]
    h = _rms(x, gmix_ref[...]).astype(BF16)
    proj = _dot(h, win_ref[...])

    cos = cos_ref[...]
    sin = sin_ref[...]
    for hd in range(RET_HEADS):
        lanes = slice(hd * HEAD_DIM, (hd + 1) * HEAD_DIM)
        q = proj[:, hd * HEAD_DIM:(hd + 1) * HEAD_DIM]
        k = proj[:, RET_WIDTH + hd * HEAD_DIM:RET_WIDTH + (hd + 1) * HEAD_DIM]
        v = proj[:, 2 * RET_WIDTH + hd * HEAD_DIM:2 * RET_WIDTH + (hd + 1) * HEAD_DIM]
        g_ret = proj[:, 3 * RET_WIDTH + hd * HEAD_DIM:3 * RET_WIDTH + (hd + 1) * HEAD_DIM]
        q = q * cos + pltpu.roll(q, HEAD_DIM // 2, 1) * sin
        k = (k * cos + pltpu.roll(k, HEAD_DIM // 2, 1) * sin) * (HEAD_DIM ** -0.5)
        vb = v.astype(BF16)
        scores = lax.dot_general(q.astype(BF16), k.astype(BF16), (((1,), (1,)), ((), ())),
                                 preferred_element_type=F32) * dec_ref[hd]
        state = r_ref[hd]
        ret = _dot(scores.astype(BF16), vb) + _dot((q * qdec_ref[:, lanes]).astype(BF16), state.astype(BF16))
        kv = lax.dot_general((k * kdec_ref[:, lanes]).astype(BF16), vb, (((0,), (0,)), ((), ())),
                             preferred_element_type=F32)
        r_ref[hd] = state * cdec_ref[:, lanes] + kv
        mu = jnp.mean(ret, axis=-1, keepdims=True)
        cen = ret - mu
        var = jnp.mean(cen * cen, axis=-1, keepdims=True)
        normed = cen * lax.rsqrt(var + EPS) * rg_ref[:, lanes]
        merged_ref[:, lanes] = (normed * jax.nn.silu(g_ret)).astype(BF16)

    x_lru = proj[:, 4 * RET_WIDTH:4 * RET_WIDTH + LRU_WIDTH]
    g_lru = proj[:, 4 * RET_WIDTH + LRU_WIDTH:]
    xl_ref[8:8 + tc, :] = x_lru
    xc = cb_ref[...]
    for tap in range(CONV_WIDTH):
        start = 8 - (CONV_WIDTH - 1) + tap
        xc = xc + xl_ref[start:start + tc, :] * cw_ref[tap:tap + 1, :]
    xl_ref[0:8, :] = xl_ref[tc:tc + 8, :]
    lam = lam_ref[...]
    softplus_neg = jnp.maximum(-lam, 0.0) + jnp.log1p(jnp.exp(-jnp.abs(lam)))
    rows = lax.broadcasted_iota(jnp.int32, (tc, LRU_BLOCK_DIM), 0)
    for blk in range(LRU_BLOCKS):
        lanes = slice(blk * LRU_BLOCK_DIM, (blk + 1) * LRU_BLOCK_DIM)
        xb = xc[:, lanes]
        gates = _dot(xb.astype(BF16), wg_ref[blk])
        r = jax.nn.sigmoid(gates[:, :LRU_BLOCK_DIM] + ba_ref[:, lanes])
        i = jax.nn.sigmoid(gates[:, LRU_BLOCK_DIM:] + bi_ref[:, lanes])
        log_a = -LRU_C * r * softplus_neg[:, lanes]
        a = jnp.exp(log_a)
        mult = jnp.sqrt((1.0 + a * a) * jnp.tanh(-log_a))
        b = mult * i * xb
        d = 1
        while d < tc:
            keep = rows >= d
            a_prev = jnp.where(keep, pltpu.roll(a, d, 0), 1.0)
            b_prev = jnp.where(keep, pltpu.roll(b, d, 0), 0.0)
            b = a * b_prev + b
            a = a * a_prev
            d *= 2
        hseq = a * hc_ref[:, lanes] + b
        hc_ref[:, lanes] = hseq[tc - 1:tc, :]
        lru_lanes = slice(RET_WIDTH + blk * LRU_BLOCK_DIM, RET_WIDTH + (blk + 1) * LRU_BLOCK_DIM)
        merged_ref[:, lru_lanes] = (hseq * jax.nn.gelu(g_lru[:, lanes])).astype(BF16)

    out_ref[...] = x + _dot(merged_ref[...], wout_ref[...])


def _rope_tables(seq):
    half = HEAD_DIM // 2
    inv = ROPE_BASE ** (-jnp.arange(half, dtype=F32) / half)
    ang = jnp.arange(seq, dtype=F32)[:, None] * inv[None, :]
    cos = jnp.cos(ang)
    sin = jnp.sin(ang)
    return jnp.concatenate([cos, cos], axis=-1), jnp.concatenate([-sin, sin], axis=-1)


def _decay_tables(chunk):
    log_gamma = jnp.log1p(-jnp.power(2.0, -5.0 - jnp.arange(RET_HEADS, dtype=F32)))
    pos = jnp.arange(chunk, dtype=F32)
    diff = pos[:, None] - pos[None, :]
    decay = jnp.where(diff >= 0, jnp.exp(log_gamma[:, None, None] * jnp.maximum(diff, 0.0)), 0.0)
    k_decay = jnp.exp(log_gamma[:, None] * (chunk - 1.0 - pos)[None, :])
    q_decay = jnp.exp(log_gamma[:, None] * (pos + 1.0)[None, :])
    chunk_decay = jnp.exp(log_gamma * chunk)

    def lanes(t):
        return jnp.repeat(t.T, HEAD_DIM, axis=1)

    return decay, lanes(q_decay), lanes(k_decay), lanes(chunk_decay[:, None])


def _even_mixer(x, batch, seq, g_mix, w_in, w_out, ret_g, conv_w, conv_b, w_gates, b_a, b_i, lam):
    tc = MIX_ROWS
    tiles = seq // tc
    cos, sin = _rope_tables(seq)
    decay, q_decay, k_decay, chunk_decay = _decay_tables(tc)
    row_spec = pl.BlockSpec((tc, D_MODEL), lambda b, j: (b * tiles + j, 0))
    pos_spec = pl.BlockSpec((tc, HEAD_DIM), lambda b, j: (j, 0))
    return pl.pallas_call(
        _even_body,
        grid=(batch, tiles),
        in_specs=[row_spec, _const_spec((1, D_MODEL)), _const_spec((D_MODEL, IN_EVEN_WIDTH)),
                  _const_spec((D_MODEL, D_MODEL)), _const_spec((1, RET_WIDTH)),
                  _const_spec((CONV_WIDTH, LRU_WIDTH)), _const_spec((1, LRU_WIDTH)),
                  _const_spec((LRU_BLOCKS, LRU_BLOCK_DIM, 2 * LRU_BLOCK_DIM)),
                  _const_spec((1, LRU_WIDTH)), _const_spec((1, LRU_WIDTH)), _const_spec((1, LRU_WIDTH)),
                  pos_spec, pos_spec, _const_spec((RET_HEADS, tc, tc)),
                  _const_spec((tc, RET_WIDTH)), _const_spec((tc, RET_WIDTH)), _const_spec((1, RET_WIDTH))],
        out_specs=row_spec,
        out_shape=jax.ShapeDtypeStruct(x.shape, F32),
        scratch_shapes=[pltpu.VMEM((RET_HEADS, HEAD_DIM, HEAD_DIM), F32),
                        pltpu.VMEM((1, LRU_WIDTH), F32),
                        pltpu.VMEM((tc + 8, LRU_WIDTH), F32),
                        pltpu.VMEM((tc, D_MODEL), BF16)],
        compiler_params=pltpu.CompilerParams(dimension_semantics=("parallel", "arbitrary"),
                                             vmem_limit_bytes=VMEM_LIMIT_BYTES),
        name="even_mixer",
    )(x, g_mix, w_in, w_out, ret_g, conv_w, conv_b, w_gates, b_a, b_i, lam,
      cos, sin, decay, q_decay, k_decay, chunk_decay)


def _s5_discretize(lam_re, lam_im, log_dt):
    dt = jnp.exp(log_dt)
    mag = jnp.exp(lam_re * dt)
    lbar_re = mag * jnp.cos(lam_im * dt)
    lbar_im = mag * jnp.sin(lam_im * dt)
    den = lam_re * lam_re + lam_im * lam_im
    nr = lbar_re - 1.0
    ni = lbar_im
    f_re = (nr * lam_re + ni * lam_im) / den
    f_im = (ni * lam_re - nr * lam_im) / den
    return lbar_re, lbar_im, f_re, f_im


def _s5_ops_body(lre_row_ref, lim_row_ref, ldt_row_ref, lre_col_ref, lim_col_ref, ldt_col_ref,
                 bt_re_ref, bt_im_ref, ct_re_ref, ct_im_ref,
                 a_re_ref, a_im_ref, k_ref, mo_re_ref, mo_im_ref, l8_re_ref, l8_im_ref):
    lr, li, f_re, f_im = _s5_discretize(lre_row_ref[...], lim_row_ref[...], ldt_row_ref[...])
    bt_re = bt_re_ref[...]
    bt_im = bt_im_ref[...]
    bbar_re = f_re * bt_re - f_im * bt_im
    bbar_im = f_re * bt_im + f_im * bt_re
    ct_re = ct_re_ref[...]
    ct_im = ct_im_ref[...]
    pr = jnp.ones_like(lr)
    pi = jnp.zeros_like(li)
    for tau in range(S5_STEP):
        a_re = bbar_re * pr - bbar_im * pi
        a_im = bbar_re * pi + bbar_im * pr
        a_re_ref[:, tau] = a_re
        a_im_ref[:, tau] = a_im
        for gi in range(S5_OPS_GROUPS):
            k_ref[gi, tau] = (
                jnp.dot(a_re[gi], ct_re[gi], precision=lax.Precision.HIGHEST, preferred_element_type=F32)
                - jnp.dot(a_im[gi], ct_im[gi], precision=lax.Precision.HIGHEST, preferred_element_type=F32))
        pr, pi = pr * lr - pi * li, pr * li + pi * lr
    l8_re_ref[...] = pr
    l8_im_ref[...] = pi

    lr_c, li_c, _, _ = _s5_discretize(lre_col_ref[...], lim_col_ref[...], ldt_col_ref[...])
    qr = lr_c
    qi = li_c
    for t in range(S5_STEP):
        mo_re_ref[:, t] = ct_re * qr - ct_im * qi
        mo_im_ref[:, t] = -(ct_re * qi) - ct_im * qr
        qr, qi = qr * lr_c - qi * li_c, qr * li_c + qi * lr_c


def _s5_operators(lam_re, lam_im, log_dt, b_re, b_im, c_re, c_im):
    g, p, c = S5_GROUPS, S5_STATE, S5_GROUP
    ldt = jnp.broadcast_to(log_dt[:, None], (g, p))
    rows = [a.reshape(g, 1, p) for a in (lam_re, lam_im, ldt)]
    cols = [a.reshape(g, p, 1) for a in (lam_re, lam_im, ldt)]
    bt = [jnp.swapaxes(a, 1, 2) for a in (b_re, b_im)]
    ct = [jnp.swapaxes(a, 1, 2) for a in (c_re, c_im)]

    def spec(*shape):
        nd = len(shape)
        return pl.BlockSpec((S5_OPS_GROUPS,) + shape, lambda i: (i,) + (0,) * nd)

    a_re, a_im, kt, mo_re, mo_im, l8_re, l8_im = pl.pallas_call(
        _s5_ops_body,
        grid=(g // S5_OPS_GROUPS,),
        in_specs=[spec(1, p)] * 3 + [spec(p, 1)] * 3 + [spec(c, p)] * 2 + [spec(p, c)] * 2,
        out_specs=[spec(S5_STEP, c, p), spec(S5_STEP, c, p), spec(S5_STEP, c, c),
                   spec(S5_STEP, p, c), spec(S5_STEP, p, c), spec(1, p), spec(1, p)],
        out_shape=[jax.ShapeDtypeStruct((g, S5_STEP, c, p), F32), jax.ShapeDtypeStruct((g, S5_STEP, c, p), F32),
                   jax.ShapeDtypeStruct((g, S5_STEP, c, c), F32),
                   jax.ShapeDtypeStruct((g, S5_STEP, p, c), F32), jax.ShapeDtypeStruct((g, S5_STEP, p, c), F32),
                   jax.ShapeDtypeStruct((g, 1, p), F32), jax.ShapeDtypeStruct((g, 1, p), F32)],
        compiler_params=pltpu.CompilerParams(dimension_semantics=("parallel",)),
        name="s5_operators",
    )(*rows, *cols, *bt, *ct)

    n = S5_STEP * c
    m_in = jnp.concatenate([a_re[:, ::-1], a_im[:, ::-1]], axis=-1).reshape(g, n, 2 * p)
    lag = jnp.arange(S5_STEP)[None, :] - jnp.arange(S5_STEP)[:, None]
    blocks = jnp.where((lag >= 0)[None, :, :, None, None], kt[:, jnp.maximum(lag, 0)], 0.0)
    m_intra = blocks.transpose(0, 1, 3, 2, 4).reshape(g, n, n)
    m_out = jnp.concatenate([mo_re.transpose(0, 2, 1, 3).reshape(g, p, n),
                             mo_im.transpose(0, 2, 1, 3).reshape(g, p, n)], axis=1)

    def pair(m, split_rows, split_cols):
        m = m.reshape(g // 2, 2, n, n)
        z = jnp.zeros_like(m[:, 0])
        full = jnp.concatenate([jnp.concatenate([m[:, 0], z], axis=2),
                                jnp.concatenate([z, m[:, 1]], axis=2)], axis=1)
        order = jnp.concatenate([jnp.arange(p), 2 * p + jnp.arange(p), p + jnp.arange(p), 3 * p + jnp.arange(p)])
        if split_rows:
            full = full[:, order, :]
        if split_cols:
            full = full[:, :, order]
        return full.astype(BF16)

    l8 = jnp.concatenate([l8_re.reshape(g // 2, 1, 2 * p), l8_im.reshape(g // 2, 1, 2 * p)], axis=-1)
    return pair(m_in, False, True), pair(m_intra, False, False), pair(m_out, True, False), l8


def _s5_body(batch, u_ref, min_ref, mintra_ref, mout_ref, l8_ref, y_ref, s_ref, h_ref):
    rows = u_ref.shape[0]
    half = S5_PAIR // 2
    pairs = S5_LANES // S5_PAIR
    for pp in range(pairs):
        lanes = slice(pp * S5_PAIR, (pp + 1) * S5_PAIR)
        s_ref[:, lanes] = _dot(u_ref[:, lanes], min_ref[pp])

    l8 = [(l8_ref[pp, :, :half], l8_ref[pp, :, half:]) for pp in range(pairs)]

    def step(n, carry):
        r0 = pl.multiple_of(n * batch, batch)
        new = []
        for pp in range(pairs):
            st_re, st_im = carry[pp]
            re_lanes = slice(pp * S5_PAIR, pp * S5_PAIR + half)
            im_lanes = slice(pp * S5_PAIR + half, (pp + 1) * S5_PAIR)
            h_ref[pl.ds(r0, batch), re_lanes] = st_re.astype(BF16)
            h_ref[pl.ds(r0, batch), im_lanes] = st_im.astype(BF16)
            lr, li = l8[pp]
            new.append((lr * st_re - li * st_im + s_ref[pl.ds(r0, batch), re_lanes],
                        lr * st_im + li * st_re + s_ref[pl.ds(r0, batch), im_lanes]))
        return tuple(new)

    zero = jnp.zeros((batch, half), F32)
    lax.fori_loop(0, rows // batch, step, tuple((zero, zero) for _ in range(pairs)))

    for pp in range(pairs):
        lanes = slice(pp * S5_PAIR, (pp + 1) * S5_PAIR)
        y_ref[:, lanes] = _dot(u_ref[:, lanes], mintra_ref[pp]) + _dot(h_ref[:, lanes], mout_ref[pp])


def _s5_scan(u, batch, m_in, m_intra, m_out, l8):
    rows, width = u.shape
    pairs = S5_LANES // S5_PAIR
    lane_spec = pl.BlockSpec((rows, S5_LANES), lambda i: (0, i))
    op_spec = pl.BlockSpec((pairs, S5_PAIR, S5_PAIR), lambda i: (i, 0, 0))
    return pl.pallas_call(
        functools.partial(_s5_body, batch),
        grid=(width // S5_LANES,),
        in_specs=[lane_spec, op_spec, op_spec, op_spec, pl.BlockSpec((pairs, 1, S5_PAIR), lambda i: (i, 0, 0))],
        out_specs=lane_spec,
        out_shape=jax.ShapeDtypeStruct((rows, width), F32),
        scratch_shapes=[pltpu.VMEM((rows, S5_LANES), F32), pltpu.VMEM((rows, S5_LANES), BF16)],
        compiler_params=pltpu.CompilerParams(dimension_semantics=("parallel",),
                                             vmem_limit_bytes=VMEM_LIMIT_BYTES),
        name="s5_scan",
    )(u, m_in, m_intra, m_out, l8)


def _glu_body(x_ref, ys_ref, gmix_ref, d_ref, wab_ref, out_ref, scr_ref):
    x = x_ref[...]
    chunks = x.shape[0] // S5_STEP
    for lb in range(D_MODEL // 128):
        groups = [ys_ref[:, (lb * 8 + g8) * 128:(lb * 8 + g8 + 1) * 128] for g8 in range(8)]
        for s, blk in enumerate(_block_transpose(groups)):
            scr_ref[lb, pl.ds(s, chunks, stride=S5_STEP), :] = blk
    ys = jnp.concatenate([scr_ref[lb] for lb in range(D_MODEL // 128)], axis=1)
    u = _rms(x, gmix_ref[...])
    y = ys + d_ref[...] * u
    z = jax.nn.gelu(y).astype(BF16)
    ab = _dot(z, wab_ref[...])
    out_ref[...] = x + ab[:, :D_MODEL] * jax.nn.sigmoid(ab[:, D_MODEL:])


def _glu(x, ys, g_mix, d, w_ab, seq):
    t = x.shape[0]
    tiles = seq // GLU_ROWS
    row_spec = pl.BlockSpec((GLU_ROWS, D_MODEL), lambda i: (i, 0))
    ys_spec = pl.BlockSpec((GLU_ROWS // S5_STEP, S5_STEP * D_MODEL), lambda i: (i % tiles, i // tiles))
    return pl.pallas_call(
        _glu_body,
        grid=(t // GLU_ROWS,),
        in_specs=[row_spec, ys_spec, _const_spec((1, D_MODEL)), _const_spec((1, D_MODEL)),
                  _const_spec((D_MODEL, 2 * D_MODEL))],
        out_specs=row_spec,
        out_shape=jax.ShapeDtypeStruct(x.shape, F32),
        scratch_shapes=[pltpu.VMEM((D_MODEL // 128, GLU_ROWS, 128), F32)],
        compiler_params=pltpu.CompilerParams(dimension_semantics=("parallel",),
                                             vmem_limit_bytes=VMEM_LIMIT_BYTES),
        name="glu",
    )(x, ys, g_mix, d, w_ab)


def kernel(x, ffn_norm_g, ffn_w1, ffn_w3, ffn_w2, mix_norm_g, w_in_even, w_out_even, ret_norm_g, conv_w, conv_b, lru_w_a, lru_b_a, lru_w_i, lru_b_i, lru_lambda, s5_lambda_re, s5_lambda_im, s5_log_dt, s5_b_re, s5_b_im, s5_c_re, s5_c_im, s5_d, glu_w_a, glu_w_b, final_norm_g):
    batch, seq, _ = x.shape
    assert seq % MIX_ROWS == 0 and seq % FFN_ROWS == 0 and seq % GLU_ROWS == 0
    assert (batch * S5_STEP * D_MODEL) % S5_LANES == 0 and S5_GROUPS % S5_OPS_GROUPS == 0
    t = batch * seq
    row = lambda v: v.reshape(1, -1).astype(F32)
    xf = x.reshape(t, D_MODEL)

    def ffn(xv, layer, half, g2, post):
        return _ffn(xv, row(ffn_norm_g[layer, half]), ffn_w1[layer, half].astype(BF16),
                    ffn_w3[layer, half].astype(BF16), ffn_w2[layer, half].astype(BF16), row(g2), post, seq)

    (xf,) = ffn(xf, 0, 0, final_norm_g, "plain")
    w_gates = jnp.concatenate([lru_w_a[0], lru_w_i[0]], axis=-1).astype(BF16)
    xf = _even_mixer(xf, batch, seq, row(mix_norm_g[0]), w_in_even[0].astype(BF16), w_out_even[0].astype(BF16),
                     row(ret_norm_g[0]), conv_w[0].astype(F32), row(conv_b[0]), w_gates,
                     row(lru_b_a[0]), row(lru_b_i[0]), row(lru_lambda[0]))
    (xf,) = ffn(xf, 0, 1, final_norm_g, "plain")

    xf, u = ffn(xf, 1, 0, mix_norm_g[1], "chunked")
    chunks = seq // S5_STEP
    m_in, m_intra, m_out, l8 = _s5_operators(s5_lambda_re[0], s5_lambda_im[0], s5_log_dt[0],
                                             s5_b_re[0], s5_b_im[0], s5_c_re[0], s5_c_im[0])
    ys = _s5_scan(u.reshape(chunks * batch, S5_STEP * D_MODEL), batch, m_in, m_intra, m_out, l8)
    w_ab = jnp.concatenate([glu_w_a[0], glu_w_b[0]], axis=-1).astype(BF16)
    xf = _glu(xf, ys.reshape(chunks, batch * S5_STEP * D_MODEL), row(mix_norm_g[1]), row(s5_d[0]), w_ab, seq)
    (out,) = ffn(xf, 1, 1, final_norm_g, "final")
    return out.reshape(batch, seq, D_MODEL)
```

```python
import functools
import math

import jax
import jax.numpy as jnp
from jax import lax
from jax.experimental import pallas as pl
from jax.experimental.pallas import tpu as pltpu

F32 = jnp.float32
BF16 = jnp.bfloat16

D_MODEL = 1024
D_FF = 2816
EPS = 1e-6
RET_HEADS = 4
HEAD_DIM = 128
RET_WIDTH = RET_HEADS * HEAD_DIM
ROPE_BASE = 10000.0
LRU_WIDTH = 512
LRU_BLOCKS = 4
LRU_BLOCK_DIM = 128
CONV_WIDTH = 4
LRU_C = 8.0
IN_EVEN_WIDTH = 4 * RET_WIDTH + 2 * LRU_WIDTH
S5_GROUP = 16
S5_GROUPS = 64
S5_STATE = 64
S5_STEP = 8
S5_PAIR = 2 * S5_STEP * S5_GROUP

VMEM_LIMIT_BYTES = 56 * 1024 * 1024

FFN_ROWS = 512
FFN_COLS = 256
MIX_ROWS = 256
GLU_ROWS = 512
S5_LANES = 512
S5_OPS_GROUPS = 8
PITCH_PAD = 8


def _rms(x, g):
    return x * lax.rsqrt(jnp.mean(x * x, axis=-1, keepdims=True) + EPS) * g


def _dot(a, b):
    return jnp.dot(a, b, preferred_element_type=F32)


def _const_spec(shape):
    nd = len(shape)
    return pl.BlockSpec(shape, lambda *_: (0,) * nd, pipeline_mode=pl.Buffered(1))


def _block_transpose(v):
    lane_block = lax.broadcasted_iota(jnp.int32, v[0].shape, 1) // S5_GROUP
    for k in (4, 2, 1):
        upper = (lane_block & k) != 0
        new = list(v)
        for a in range(8):
            if a & k == 0:
                new[a] = jnp.where(upper, pltpu.roll(v[a + k], S5_GROUP * k, 1), v[a])
                new[a + k] = jnp.where(upper, v[a + k], pltpu.roll(v[a], 128 - S5_GROUP * k, 1))
        v = new
    return v


def _ffn_body(post, x_ref, g_ref, w1_ref, w3_ref, w2_ref, g2_ref, *out_refs):
    x = x_ref[...].reshape(-1, D_MODEL)
    xn = _rms(x, g_ref[...]).astype(BF16)
    acc = jnp.zeros(x.shape, F32)
    for f in range(0, D_FF, FFN_COLS):
        h1 = _dot(xn, w1_ref[:, f:f + FFN_COLS])
        h3 = _dot(xn, w3_ref[:, f:f + FFN_COLS])
        gate = (jax.nn.silu(h1) * h3).astype(BF16)
        acc = acc + _dot(gate, w2_ref[f:f + FFN_COLS, :])
    y = x + 0.5 * acc
    if post == "final":
        out_refs[0][...] = _rms(y, g2_ref[...])
    else:
        out_refs[0][...] = y.reshape(out_refs[0].shape)
        if post == "chunked":
            u_ref, scr_ref = out_refs[1], out_refs[2]
            batch, steps = x_ref.shape[0], x_ref.shape[1]
            pitch = steps + PITCH_PAD
            hn = _rms(y, g2_ref[...])
            for lb in range(D_MODEL // 128):
                for b in range(batch):
                    scr_ref[lb, b * pitch:b * pitch + steps, :] = hn[b * steps:(b + 1) * steps, lb * 128:(lb + 1) * 128]
                for c in range(steps // S5_STEP):
                    per_step = [scr_ref[lb, pl.ds(c * S5_STEP + s, batch, stride=pitch), :]
                                for s in range(S5_STEP)]
                    for g8, blk in enumerate(_block_transpose(per_step)):
                        lane0 = (lb * 8 + g8) * 128
                        u_ref[c, :, lane0:lane0 + 128] = blk.astype(BF16)


def _ffn(x, g, w1, w3, w2, g2, post, layer, half):
    scratch = []
    if post == "chunked":
        batch, seq, _ = x.shape
        steps = FFN_ROWS // batch
        grid = seq // steps
        row_spec = pl.BlockSpec((batch, steps, D_MODEL), lambda i: (0, i, 0))
        out_shape = [jax.ShapeDtypeStruct(x.shape, F32),
                     jax.ShapeDtypeStruct((seq // S5_STEP, batch, S5_STEP * D_MODEL), BF16)]
        out_specs = [row_spec, pl.BlockSpec((steps // S5_STEP, batch, S5_STEP * D_MODEL), lambda i: (i, 0, 0))]
        scratch.append(pltpu.VMEM((D_MODEL // 128, batch * (steps + PITCH_PAD), 128), F32))
    else:
        grid = x.shape[0] // FFN_ROWS
        row_spec = pl.BlockSpec((FFN_ROWS, D_MODEL), lambda i: (i, 0))
        out_shape = [jax.ShapeDtypeStruct(x.shape, F32)]
        out_specs = [row_spec]

    def weight_spec(rows, cols):
        return pl.BlockSpec((None, None, rows, cols), lambda i: (layer, half, 0, 0), pipeline_mode=pl.Buffered(1))

    return pl.pallas_call(
        functools.partial(_ffn_body, post),
        grid=(grid,),
        in_specs=[row_spec, _const_spec((1, D_MODEL)), weight_spec(D_MODEL, D_FF), weight_spec(D_MODEL, D_FF),
                  weight_spec(D_FF, D_MODEL), _const_spec((1, D_MODEL))],
        out_specs=out_specs,
        out_shape=out_shape,
        scratch_shapes=scratch,
        compiler_params=pltpu.CompilerParams(dimension_semantics=("parallel",),
                                             vmem_limit_bytes=VMEM_LIMIT_BYTES),
        name="ffn_" + post,
    )(x, g, w1, w3, w2, g2)


def _even_body(x_ref, gmix_ref, win_ref, wout_ref, rg_ref, cw_ref, cb_ref, wg_ref, ba_ref, bi_ref,
               lam_ref, cos_ref, sin_ref, dec_ref, qdec_ref, kdec_ref, cdec_ref,
               out_ref, r_ref, hc_ref, xl_ref, merged_ref):
    tc = x_ref.shape[0]

    @pl.when(pl.program_id(1) == 0)
    def _():
        r_ref[...] = jnp.zeros(r_ref.shape, F32)
        hc_ref[...] = jnp.zeros(hc_ref.shape, F32)
        xl_ref[0:8, :] = jnp.zeros((8, LRU_WIDTH), F32)

    x = x_ref[...]
    h = _rms(x, gmix_ref[...]).astype(BF16)
    proj = _dot(h, win_ref[...])

    cos = cos_ref[...]
    sin = sin_ref[...]
    for hd in range(RET_HEADS):
        lanes = slice(hd * HEAD_DIM, (hd + 1) * HEAD_DIM)
        q = proj[:, hd * HEAD_DIM:(hd + 1) * HEAD_DIM]
        k = proj[:, RET_WIDTH + hd * HEAD_DIM:RET_WIDTH + (hd + 1) * HEAD_DIM]
        v = proj[:, 2 * RET_WIDTH + hd * HEAD_DIM:2 * RET_WIDTH + (hd + 1) * HEAD_DIM]
        g_ret = proj[:, 3 * RET_WIDTH + hd * HEAD_DIM:3 * RET_WIDTH + (hd + 1) * HEAD_DIM]
        q = q * cos + pltpu.roll(q, HEAD_DIM // 2, 1) * sin
        k = (k * cos + pltpu.roll(k, HEAD_DIM // 2, 1) * sin) * (HEAD_DIM ** -0.5)
        vb = v.astype(BF16)
        scores = lax.dot_general(q.astype(BF16), k.astype(BF16), (((1,), (1,)), ((), ())),
                                 preferred_element_type=F32) * dec_ref[hd]
        state = r_ref[hd]
        ret = _dot(scores.astype(BF16), vb) + _dot((q * qdec_ref[:, lanes]).astype(BF16), state.astype(BF16))
        kv = lax.dot_general((k * kdec_ref[:, lanes]).astype(BF16), vb, (((0,), (0,)), ((), ())),
                             preferred_element_type=F32)
        r_ref[hd] = state * cdec_ref[:, lanes] + kv
        mu = jnp.mean(ret, axis=-1, keepdims=True)
        cen = ret - mu
        var = jnp.mean(cen * cen, axis=-1, keepdims=True)
        normed = cen * lax.rsqrt(var + EPS) * rg_ref[:, lanes]
        merged_ref[:, lanes] = (normed * jax.nn.silu(g_ret)).astype(BF16)

    x_lru = proj[:, 4 * RET_WIDTH:4 * RET_WIDTH + LRU_WIDTH]
    g_lru = proj[:, 4 * RET_WIDTH + LRU_WIDTH:]
    xl_ref[8:8 + tc, :] = x_lru
    xc = cb_ref[...]
    for tap in range(CONV_WIDTH):
        start = 8 - (CONV_WIDTH - 1) + tap
        xc = xc + xl_ref[start:start + tc, :] * cw_ref[tap:tap + 1, :]
    xl_ref[0:8, :] = xl_ref[tc:tc + 8, :]
    lam = lam_ref[...]
    softplus_neg = jnp.maximum(-lam, 0.0) + jnp.log1p(jnp.exp(-jnp.abs(lam)))
    rows = lax.broadcasted_iota(jnp.int32, (tc // 8, 8, LRU_BLOCK_DIM), 1)
    for blk in range(LRU_BLOCKS):
        lanes = slice(blk * LRU_BLOCK_DIM, (blk + 1) * LRU_BLOCK_DIM)
        xb = xc[:, lanes]
        gates = _dot(xb.astype(BF16), wg_ref[blk])
        r = jax.nn.sigmoid(gates[:, :LRU_BLOCK_DIM] + ba_ref[:, lanes])
        i = jax.nn.sigmoid(gates[:, LRU_BLOCK_DIM:] + bi_ref[:, lanes])
        log_a = -LRU_C * r * softplus_neg[:, lanes]
        a = jnp.exp(log_a)
        mult = jnp.sqrt((1.0 + a * a) * jnp.tanh(-log_a))
        a = a.reshape(tc // 8, 8, LRU_BLOCK_DIM)
        b = (mult * i * xb).reshape(tc // 8, 8, LRU_BLOCK_DIM)
        for d in (1, 2, 4):
            keep = rows >= d
            a_prev = jnp.where(keep, pltpu.roll(a, d, 1), 1.0)
            b_prev = jnp.where(keep, pltpu.roll(b, d, 1), 0.0)
            b = a * b_prev + b
            a = a * a_prev
        carry = hc_ref[:, lanes]
        groups = []
        for j in range(tc // 8):
            hj = a[j] * carry + b[j]
            carry = hj[7:8, :]
            groups.append(hj)
        hc_ref[:, lanes] = carry
        hseq = jnp.concatenate(groups, axis=0)
        lru_lanes = slice(RET_WIDTH + blk * LRU_BLOCK_DIM, RET_WIDTH + (blk + 1) * LRU_BLOCK_DIM)
        merged_ref[:, lru_lanes] = (hseq * jax.nn.gelu(g_lru[:, lanes])).astype(BF16)

    out_ref[...] = x + _dot(merged_ref[...], wout_ref[...])


def _rope_tables(seq):
    half = HEAD_DIM // 2
    inv = ROPE_BASE ** (-jnp.arange(half, dtype=F32) / half)
    ang = jnp.arange(seq, dtype=F32)[:, None] * inv[None, :]
    cos = jnp.cos(ang)
    sin = jnp.sin(ang)
    return jnp.concatenate([cos, cos], axis=-1), jnp.concatenate([-sin, sin], axis=-1)


def _decay_tables(chunk):
    log_gamma = jnp.log1p(-jnp.power(2.0, -5.0 - jnp.arange(RET_HEADS, dtype=F32)))
    pos = jnp.arange(chunk, dtype=F32)
    diff = pos[:, None] - pos[None, :]
    decay = jnp.where(diff >= 0, jnp.exp(log_gamma[:, None, None] * jnp.maximum(diff, 0.0)), 0.0)
    k_decay = jnp.exp(log_gamma[:, None] * (chunk - 1.0 - pos)[None, :])
    q_decay = jnp.exp(log_gamma[:, None] * (pos + 1.0)[None, :])
    chunk_decay = jnp.exp(log_gamma * chunk)

    def lanes(t):
        return jnp.repeat(t.T, HEAD_DIM, axis=1)

    return decay, lanes(q_decay), lanes(k_decay), lanes(chunk_decay[:, None])


def _even_mixer(x, batch, seq, g_mix, w_in, w_out, ret_g, conv_w, conv_b, w_gates, b_a, b_i, lam):
    tc = MIX_ROWS
    tiles = seq // tc
    cos, sin = _rope_tables(seq)
    decay, q_decay, k_decay, chunk_decay = _decay_tables(tc)
    row_spec = pl.BlockSpec((tc, D_MODEL), lambda b, j: (b * tiles + j, 0))
    pos_spec = pl.BlockSpec((tc, HEAD_DIM), lambda b, j: (j, 0))
    return pl.pallas_call(
        _even_body,
        grid=(batch, tiles),
        in_specs=[row_spec, _const_spec((1, D_MODEL)), _const_spec((D_MODEL, IN_EVEN_WIDTH)),
                  _const_spec((D_MODEL, D_MODEL)), _const_spec((1, RET_WIDTH)),
                  _const_spec((CONV_WIDTH, LRU_WIDTH)), _const_spec((1, LRU_WIDTH)),
                  _const_spec((LRU_BLOCKS, LRU_BLOCK_DIM, 2 * LRU_BLOCK_DIM)),
                  _const_spec((1, LRU_WIDTH)), _const_spec((1, LRU_WIDTH)), _const_spec((1, LRU_WIDTH)),
                  pos_spec, pos_spec, _const_spec((RET_HEADS, tc, tc)),
                  _const_spec((tc, RET_WIDTH)), _const_spec((tc, RET_WIDTH)), _const_spec((1, RET_WIDTH))],
        out_specs=row_spec,
        out_shape=jax.ShapeDtypeStruct(x.shape, F32),
        scratch_shapes=[pltpu.VMEM((RET_HEADS, HEAD_DIM, HEAD_DIM), F32),
                        pltpu.VMEM((1, LRU_WIDTH), F32),
                        pltpu.VMEM((tc + 8, LRU_WIDTH), F32),
                        pltpu.VMEM((tc, D_MODEL), BF16)],
        compiler_params=pltpu.CompilerParams(dimension_semantics=("parallel", "arbitrary"),
                                             vmem_limit_bytes=VMEM_LIMIT_BYTES),
        name="even_mixer",
    )(x, g_mix, w_in, w_out, ret_g, conv_w, conv_b, w_gates, b_a, b_i, lam,
      cos, sin, decay, q_decay, k_decay, chunk_decay)


def _s5_discretize(lam_re, lam_im, log_dt):
    dt = jnp.exp(log_dt)
    mag = jnp.exp(lam_re * dt)
    lbar_re = mag * jnp.cos(lam_im * dt)
    lbar_im = mag * jnp.sin(lam_im * dt)
    den = lam_re * lam_re + lam_im * lam_im
    nr = lbar_re - 1.0
    ni = lbar_im
    f_re = (nr * lam_re + ni * lam_im) / den
    f_im = (ni * lam_re - nr * lam_im) / den
    return lbar_re, lbar_im, f_re, f_im


def _s5_ops_body(lre_row_ref, lim_row_ref, ldt_row_ref, lre_col_ref, lim_col_ref, ldt_col_ref,
                 bt_re_ref, bt_im_ref, ct_re_ref, ct_im_ref,
                 a_re_ref, a_im_ref, k_ref, mo_re_ref, mo_im_ref, l8_re_ref, l8_im_ref):
    lr, li, f_re, f_im = _s5_discretize(lre_row_ref[...], lim_row_ref[...], ldt_row_ref[...])
    bt_re = bt_re_ref[...]
    bt_im = bt_im_ref[...]
    bbar_re = f_re * bt_re - f_im * bt_im
    bbar_im = f_re * bt_im + f_im * bt_re
    ct_re = ct_re_ref[...]
    ct_im = ct_im_ref[...]
    pr = jnp.ones_like(lr)
    pi = jnp.zeros_like(li)
    for tau in range(S5_STEP):
        a_re = bbar_re * pr - bbar_im * pi
        a_im = bbar_re * pi + bbar_im * pr
        a_re_ref[:, tau] = a_re
        a_im_ref[:, tau] = a_im
        for gi in range(S5_OPS_GROUPS):
            k_ref[gi, tau] = (
                jnp.dot(a_re[gi], ct_re[gi], precision=lax.Precision.HIGHEST, preferred_element_type=F32)
                - jnp.dot(a_im[gi], ct_im[gi], precision=lax.Precision.HIGHEST, preferred_element_type=F32))
        pr, pi = pr * lr - pi * li, pr * li + pi * lr
    l8_re_ref[...] = pr
    l8_im_ref[...] = pi

    lr_c, li_c, _, _ = _s5_discretize(lre_col_ref[...], lim_col_ref[...], ldt_col_ref[...])
    qr = lr_c
    qi = li_c
    for t in range(S5_STEP):
        mo_re_ref[:, t] = ct_re * qr - ct_im * qi
        mo_im_ref[:, t] = -(ct_re * qi) - ct_im * qr
        qr, qi = qr * lr_c - qi * li_c, qr * li_c + qi * lr_c


def _s5_operators(lam_re, lam_im, log_dt, b_re, b_im, c_re, c_im):
    g, p, c = S5_GROUPS, S5_STATE, S5_GROUP
    ldt = jnp.broadcast_to(log_dt[:, None], (g, p))
    rows = [a.reshape(g, 1, p) for a in (lam_re, lam_im, ldt)]
    cols = [a.reshape(g, p, 1) for a in (lam_re, lam_im, ldt)]
    bt = [jnp.swapaxes(a, 1, 2) for a in (b_re, b_im)]
    ct = [jnp.swapaxes(a, 1, 2) for a in (c_re, c_im)]

    def spec(*shape):
        nd = len(shape)
        return pl.BlockSpec((S5_OPS_GROUPS,) + shape, lambda i: (i,) + (0,) * nd)

    a_re, a_im, kt, mo_re, mo_im, l8_re, l8_im = pl.pallas_call(
        _s5_ops_body,
        grid=(g // S5_OPS_GROUPS,),
        in_specs=[spec(1, p)] * 3 + [spec(p, 1)] * 3 + [spec(c, p)] * 2 + [spec(p, c)] * 2,
        out_specs=[spec(S5_STEP, c, p), spec(S5_STEP, c, p), spec(S5_STEP, c, c),
                   spec(S5_STEP, p, c), spec(S5_STEP, p, c), spec(1, p), spec(1, p)],
        out_shape=[jax.ShapeDtypeStruct((g, S5_STEP, c, p), F32), jax.ShapeDtypeStruct((g, S5_STEP, c, p), F32),
                   jax.ShapeDtypeStruct((g, S5_STEP, c, c), F32),
                   jax.ShapeDtypeStruct((g, S5_STEP, p, c), F32), jax.ShapeDtypeStruct((g, S5_STEP, p, c), F32),
                   jax.ShapeDtypeStruct((g, 1, p), F32), jax.ShapeDtypeStruct((g, 1, p), F32)],
        compiler_params=pltpu.CompilerParams(dimension_semantics=("parallel",)),
        name="s5_operators",
    )(*rows, *cols, *bt, *ct)

    n = S5_STEP * c
    m_in = jnp.concatenate([a_re[:, ::-1], a_im[:, ::-1]], axis=-1).reshape(g, n, 2 * p)
    lag = jnp.arange(S5_STEP)[None, :] - jnp.arange(S5_STEP)[:, None]
    blocks = jnp.where((lag >= 0)[None, :, :, None, None], kt[:, jnp.maximum(lag, 0)], 0.0)
    m_intra = blocks.transpose(0, 1, 3, 2, 4).reshape(g, n, n)
    m_out = jnp.concatenate([mo_re.transpose(0, 2, 1, 3).reshape(g, p, n),
                             mo_im.transpose(0, 2, 1, 3).reshape(g, p, n)], axis=1)

    def pair(m, split_rows, split_cols):
        m = m.reshape(g // 2, 2, n, n)
        z = jnp.zeros_like(m[:, 0])
        full = jnp.concatenate([jnp.concatenate([m[:, 0], z], axis=2),
                                jnp.concatenate([z, m[:, 1]], axis=2)], axis=1)
        order = jnp.concatenate([jnp.arange(p), 2 * p + jnp.arange(p), p + jnp.arange(p), 3 * p + jnp.arange(p)])
        if split_rows:
            full = full[:, order, :]
        if split_cols:
            full = full[:, :, order]
        return full.astype(BF16)

    l8 = jnp.concatenate([l8_re.reshape(g // 2, 1, 2 * p), l8_im.reshape(g // 2, 1, 2 * p)], axis=-1)
    return pair(m_in, False, True), pair(m_intra, False, False), pair(m_out, True, False), l8


def _s5_body(batch, u_ref, min_ref, mintra_ref, mout_ref, l8_ref, y_ref, s_ref, h_ref):
    rows = u_ref.shape[0]
    half = S5_PAIR // 2
    pairs = S5_LANES // S5_PAIR
    for pp in range(pairs):
        lanes = slice(pp * S5_PAIR, (pp + 1) * S5_PAIR)
        s_ref[:, lanes] = _dot(u_ref[:, lanes], min_ref[pp])

    l8 = [(l8_ref[pp, :, :half], l8_ref[pp, :, half:]) for pp in range(pairs)]

    def step(n, carry):
        r0 = pl.multiple_of(n * batch, batch)
        new = []
        for pp in range(pairs):
            st_re, st_im = carry[pp]
            re_lanes = slice(pp * S5_PAIR, pp * S5_PAIR + half)
            im_lanes = slice(pp * S5_PAIR + half, (pp + 1) * S5_PAIR)
            h_ref[pl.ds(r0, batch), re_lanes] = st_re.astype(BF16)
            h_ref[pl.ds(r0, batch), im_lanes] = st_im.astype(BF16)
            lr, li = l8[pp]
            new.append((lr * st_re - li * st_im + s_ref[pl.ds(r0, batch), re_lanes],
                        lr * st_im + li * st_re + s_ref[pl.ds(r0, batch), im_lanes]))
        return tuple(new)

    zero = jnp.zeros((batch, half), F32)
    lax.fori_loop(0, rows // batch, step, tuple((zero, zero) for _ in range(pairs)))

    for pp in range(pairs):
        lanes = slice(pp * S5_PAIR, (pp + 1) * S5_PAIR)
        y_ref[:, lanes] = _dot(u_ref[:, lanes], mintra_ref[pp]) + _dot(h_ref[:, lanes], mout_ref[pp])


def _s5_scan(u, batch, m_in, m_intra, m_out, l8):
    rows, width = u.shape
    pairs = S5_LANES // S5_PAIR
    lane_spec = pl.BlockSpec((rows, S5_LANES), lambda i: (0, i))
    op_spec = pl.BlockSpec((pairs, S5_PAIR, S5_PAIR), lambda i: (i, 0, 0))
    return pl.pallas_call(
        functools.partial(_s5_body, batch),
        grid=(width // S5_LANES,),
        in_specs=[lane_spec, op_spec, op_spec, op_spec, pl.BlockSpec((pairs, 1, S5_PAIR), lambda i: (i, 0, 0))],
        out_specs=lane_spec,
        out_shape=jax.ShapeDtypeStruct((rows, width), F32),
        scratch_shapes=[pltpu.VMEM((rows, S5_LANES), F32), pltpu.VMEM((rows, S5_LANES), BF16)],
        compiler_params=pltpu.CompilerParams(dimension_semantics=("parallel",),
                                             vmem_limit_bytes=VMEM_LIMIT_BYTES),
        name="s5_scan",
    )(u, m_in, m_intra, m_out, l8)


def _glu_body(x_ref, ys_ref, gmix_ref, d_ref, wab_ref, out_ref, scr_ref):
    batch, steps = x_ref.shape[0], x_ref.shape[1]
    pitch = steps + PITCH_PAD
    x = x_ref[...].reshape(-1, D_MODEL)
    slabs = []
    for lb in range(D_MODEL // 128):
        for c in range(steps // S5_STEP):
            groups = [ys_ref[c, :, (lb * 8 + g8) * 128:(lb * 8 + g8 + 1) * 128] for g8 in range(8)]
            for s, blk in enumerate(_block_transpose(groups)):
                scr_ref[lb, pl.ds(c * S5_STEP + s, batch, stride=pitch), :] = blk
        slabs.append(jnp.concatenate([scr_ref[lb, b * pitch:b * pitch + steps, :] for b in range(batch)], axis=0))
    ys = jnp.concatenate(slabs, axis=1)
    u = _rms(x, gmix_ref[...])
    y = ys + d_ref[...] * u
    z = jax.nn.gelu(y).astype(BF16)
    ab = _dot(z, wab_ref[...])
    out_ref[...] = (x + ab[:, :D_MODEL] * jax.nn.sigmoid(ab[:, D_MODEL:])).reshape(out_ref.shape)


def _glu(x, ys, g_mix, d, w_ab):
    batch, seq, _ = x.shape
    steps = GLU_ROWS // batch
    row_spec = pl.BlockSpec((batch, steps, D_MODEL), lambda i: (0, i, 0))
    ys_spec = pl.BlockSpec((steps // S5_STEP, batch, S5_STEP * D_MODEL), lambda i: (i, 0, 0))
    return pl.pallas_call(
        _glu_body,
        grid=(seq // steps,),
        in_specs=[row_spec, ys_spec, _const_spec((1, D_MODEL)), _const_spec((1, D_MODEL)),
                  _const_spec((D_MODEL, 2 * D_MODEL))],
        out_specs=row_spec,
        out_shape=jax.ShapeDtypeStruct(x.shape, F32),
        scratch_shapes=[pltpu.VMEM((D_MODEL // 128, batch * (steps + PITCH_PAD), 128), F32)],
        compiler_params=pltpu.CompilerParams(dimension_semantics=("parallel",),
                                             vmem_limit_bytes=VMEM_LIMIT_BYTES),
        name="glu",
    )(x, ys, g_mix, d, w_ab)


def kernel(x, ffn_norm_g, ffn_w1, ffn_w3, ffn_w2, mix_norm_g, w_in_even, w_out_even, ret_norm_g, conv_w, conv_b, lru_w_a, lru_b_a, lru_w_i, lru_b_i, lru_lambda, s5_lambda_re, s5_lambda_im, s5_log_dt, s5_b_re, s5_b_im, s5_c_re, s5_c_im, s5_d, glu_w_a, glu_w_b, final_norm_g):
    batch, seq, _ = x.shape
    t = batch * seq
    chunk_steps = FFN_ROWS // batch
    assert FFN_ROWS == GLU_ROWS and FFN_ROWS % batch == 0 and batch % 8 == 0
    assert seq % MIX_ROWS == 0 and t % FFN_ROWS == 0 and chunk_steps % S5_STEP == 0 and seq % chunk_steps == 0
    assert S5_GROUPS % S5_OPS_GROUPS == 0
    row = lambda v: v.reshape(1, -1).astype(F32)
    w1, w3, w2 = ffn_w1.astype(BF16), ffn_w3.astype(BF16), ffn_w2.astype(BF16)

    def ffn(xv, layer, half, g2, post):
        return _ffn(xv, row(ffn_norm_g[layer, half]), w1, w3, w2, row(g2), post, layer, half)

    (xf,) = ffn(x.reshape(t, D_MODEL), 0, 0, final_norm_g, "plain")
    w_gates = jnp.concatenate([lru_w_a[0], lru_w_i[0]], axis=-1).astype(BF16)
    xf = _even_mixer(xf, batch, seq, row(mix_norm_g[0]), w_in_even[0].astype(BF16), w_out_even[0].astype(BF16),
                     row(ret_norm_g[0]), conv_w[0].astype(F32), row(conv_b[0]), w_gates,
                     row(lru_b_a[0]), row(lru_b_i[0]), row(lru_lambda[0]))
    (xf,) = ffn(xf, 0, 1, final_norm_g, "plain")

    x3, u = ffn(xf.reshape(batch, seq, D_MODEL), 1, 0, mix_norm_g[1], "chunked")
    chunks = seq // S5_STEP
    m_in, m_intra, m_out, l8 = _s5_operators(s5_lambda_re[0], s5_lambda_im[0], s5_log_dt[0],
                                             s5_b_re[0], s5_b_im[0], s5_c_re[0], s5_c_im[0])
    ys = _s5_scan(u.reshape(chunks * batch, S5_STEP * D_MODEL), batch, m_in, m_intra, m_out, l8)
    w_ab = jnp.concatenate([glu_w_a[0], glu_w_b[0]], axis=-1).astype(BF16)
    x3 = _glu(x3, ys.reshape(chunks, batch, S5_STEP * D_MODEL), row(mix_norm_g[1]), row(s5_d[0]), w_ab)
    (out,) = ffn(x3.reshape(t, D_MODEL), 1, 1, final_norm_g, "final")
    return out.reshape(batch, seq, D_MODEL)
```

```python
import functools
import math

import jax
import jax.numpy as jnp
from jax import lax
from jax.experimental import pallas as pl
from jax.experimental.pallas import tpu as pltpu

F32 = jnp.float32
BF16 = jnp.bfloat16

D_MODEL = 1024
D_FF = 2816
EPS = 1e-6
RET_HEADS = 4
HEAD_DIM = 128
RET_WIDTH = RET_HEADS * HEAD_DIM
ROPE_BASE = 10000.0
LRU_WIDTH = 512
LRU_BLOCKS = 4
LRU_BLOCK_DIM = 128
CONV_WIDTH = 4
LRU_C = 8.0
IN_EVEN_WIDTH = 4 * RET_WIDTH + 2 * LRU_WIDTH
S5_GROUP = 16
S5_GROUPS = 64
S5_STATE = 64
S5_STEP = 8
S5_PAIR = 2 * S5_STEP * S5_GROUP

VMEM_LIMIT_BYTES = 56 * 1024 * 1024

FFN_ROWS = 512
FFN_COLS = 256
MIX_ROWS = 256
GLU_ROWS = 512
S5_LANES = 512
S5_OPS_GROUPS = 8
S5_SCAN_UNROLL = 8
PITCH_PAD = 8


def _rms(x, g):
    return x * lax.rsqrt(jnp.mean(x * x, axis=-1, keepdims=True) + EPS) * g


def _dot(a, b):
    return jnp.dot(a, b, preferred_element_type=F32)


def _const_spec(shape):
    nd = len(shape)
    return pl.BlockSpec(shape, lambda *_: (0,) * nd, pipeline_mode=pl.Buffered(1))


def _block_transpose(v):
    lane_block = lax.broadcasted_iota(jnp.int32, v[0].shape, 1) // S5_GROUP
    for k in (4, 2, 1):
        upper = (lane_block & k) != 0
        new = list(v)
        for a in range(8):
            if a & k == 0:
                new[a] = jnp.where(upper, pltpu.roll(v[a + k], S5_GROUP * k, 1), v[a])
                new[a + k] = jnp.where(upper, v[a + k], pltpu.roll(v[a], 128 - S5_GROUP * k, 1))
        v = new
    return v


def _ffn_body(post, x_ref, g_ref, w1_ref, w3_ref, w2_ref, g2_ref, *out_refs):
    x = x_ref[...].reshape(-1, D_MODEL)
    lane_blocks = D_MODEL // 128
    if post == "chunked":
        u_ref, scr_ref = out_refs[1], out_refs[2]
        batch, steps = x_ref.shape[0], x_ref.shape[1]
        pitch = steps + PITCH_PAD

        @pl.when(pl.program_id(0) == 0)
        def _():
            scr_ref[...] = jnp.zeros(scr_ref.shape, F32)

        def relayout_previous(lb):
            for c in range(steps // S5_STEP):
                per_step = [scr_ref[lb, pl.ds(c * S5_STEP + s, batch, stride=pitch), :]
                            for s in range(S5_STEP)]
                for g8, blk in enumerate(_block_transpose(per_step)):
                    lane0 = (lb * 8 + g8) * 128
                    u_ref[c, :, lane0:lane0 + 128] = blk.astype(BF16)

    xn = _rms(x, g_ref[...]).astype(BF16)
    acc = jnp.zeros(x.shape, F32)
    for ci, f in enumerate(range(0, D_FF, FFN_COLS)):
        h1 = _dot(xn, w1_ref[:, f:f + FFN_COLS])
        h3 = _dot(xn, w3_ref[:, f:f + FFN_COLS])
        if post == "chunked" and ci < lane_blocks:
            relayout_previous(ci)
        gate = (jax.nn.silu(h1) * h3).astype(BF16)
        acc = acc + _dot(gate, w2_ref[f:f + FFN_COLS, :])
    y = x + 0.5 * acc
    if post == "final":
        out_refs[0][...] = _rms(y, g2_ref[...])
    else:
        out_refs[0][...] = y.reshape(out_refs[0].shape)
        if post == "chunked":
            hn = _rms(y, g2_ref[...])
            for lb in range(lane_blocks):
                for b in range(batch):
                    scr_ref[lb, b * pitch:b * pitch + steps, :] = hn[b * steps:(b + 1) * steps, lb * 128:(lb + 1) * 128]


def _ffn(x, g, w1, w3, w2, g2, post, layer, half):
    scratch = []
    if post == "chunked":
        batch, seq, _ = x.shape
        steps = FFN_ROWS // batch
        tiles = seq // steps
        grid = tiles + 1
        semantics = "arbitrary"
        row_spec = pl.BlockSpec((batch, steps, D_MODEL), lambda i: (0, jnp.minimum(i, tiles - 1), 0))
        out_shape = [jax.ShapeDtypeStruct(x.shape, F32),
                     jax.ShapeDtypeStruct((seq // S5_STEP, batch, S5_STEP * D_MODEL), BF16)]
        out_specs = [row_spec, pl.BlockSpec((steps // S5_STEP, batch, S5_STEP * D_MODEL),
                                            lambda i: (jnp.maximum(i - 1, 0), 0, 0))]
        scratch.append(pltpu.VMEM((D_MODEL // 128, batch * (steps + PITCH_PAD), 128), F32))
    else:
        grid = x.shape[0] // FFN_ROWS
        semantics = "parallel"
        row_spec = pl.BlockSpec((FFN_ROWS, D_MODEL), lambda i: (i, 0))
        out_shape = [jax.ShapeDtypeStruct(x.shape, F32)]
        out_specs = [row_spec]

    def weight_spec(rows, cols):
        return pl.BlockSpec((None, None, rows, cols), lambda i: (layer, half, 0, 0), pipeline_mode=pl.Buffered(1))

    return pl.pallas_call(
        functools.partial(_ffn_body, post),
        grid=(grid,),
        in_specs=[row_spec, _const_spec((1, D_MODEL)), weight_spec(D_MODEL, D_FF), weight_spec(D_MODEL, D_FF),
                  weight_spec(D_FF, D_MODEL), _const_spec((1, D_MODEL))],
        out_specs=out_specs,
        out_shape=out_shape,
        scratch_shapes=scratch,
        compiler_params=pltpu.CompilerParams(dimension_semantics=(semantics,),
                                             vmem_limit_bytes=VMEM_LIMIT_BYTES),
        name="ffn_" + post,
    )(x, g, w1, w3, w2, g2)


def _even_body(x_ref, gmix_ref, win_ref, wout_ref, rg_ref, cw_ref, cb_ref, wg_ref, ba_ref, bi_ref,
               lam_ref, cos_ref, sin_ref, dec_ref, qdec_ref, kdec_ref, cdec_ref,
               out_ref, r_ref, hc_ref, xl_ref, merged_ref):
    tc = x_ref.shape[0]

    @pl.when(pl.program_id(1) == 0)
    def _():
        r_ref[...] = jnp.zeros(r_ref.shape, F32)
        hc_ref[...] = jnp.zeros(hc_ref.shape, F32)
        xl_ref[0:8, :] = jnp.zeros((8, LRU_WIDTH), F32)

    x = x_ref[...]
    h = _rms(x, gmix_ref[...]).astype(BF16)
    proj = _dot(h, win_ref[...])

    cos = cos_ref[...]
    sin = sin_ref[...]
    for hd in range(RET_HEADS):
        lanes = slice(hd * HEAD_DIM, (hd + 1) * HEAD_DIM)
        q = proj[:, hd * HEAD_DIM:(hd + 1) * HEAD_DIM]
        k = proj[:, RET_WIDTH + hd * HEAD_DIM:RET_WIDTH + (hd + 1) * HEAD_DIM]
        v = proj[:, 2 * RET_WIDTH + hd * HEAD_DIM:2 * RET_WIDTH + (hd + 1) * HEAD_DIM]
        g_ret = proj[:, 3 * RET_WIDTH + hd * HEAD_DIM:3 * RET_WIDTH + (hd + 1) * HEAD_DIM]
        q = q * cos + pltpu.roll(q, HEAD_DIM // 2, 1) * sin
        k = (k * cos + pltpu.roll(k, HEAD_DIM // 2, 1) * sin) * (HEAD_DIM ** -0.5)
        vb = v.astype(BF16)
        scores = lax.dot_general(q.astype(BF16), k.astype(BF16), (((1,), (1,)), ((), ())),
                                 preferred_element_type=F32) * dec_ref[hd]
        state = r_ref[hd]
        ret = _dot(scores.astype(BF16), vb) + _dot((q * qdec_ref[:, lanes]).astype(BF16), state.astype(BF16))
        kv = lax.dot_general((k * kdec_ref[:, lanes]).astype(BF16), vb, (((0,), (0,)), ((), ())),
                             preferred_element_type=F32)
        r_ref[hd] = state * cdec_ref[:, lanes] + kv
        mu = jnp.mean(ret, axis=-1, keepdims=True)
        cen = ret - mu
        var = jnp.mean(cen * cen, axis=-1, keepdims=True)
        normed = cen * lax.rsqrt(var + EPS) * rg_ref[:, lanes]
        merged_ref[:, lanes] = (normed * jax.nn.silu(g_ret)).astype(BF16)

    x_lru = proj[:, 4 * RET_WIDTH:4 * RET_WIDTH + LRU_WIDTH]
    g_lru = proj[:, 4 * RET_WIDTH + LRU_WIDTH:]
    xl_ref[8:8 + tc, :] = x_lru
    xc = cb_ref[...]
    for tap in range(CONV_WIDTH):
        start = 8 - (CONV_WIDTH - 1) + tap
        xc = xc + xl_ref[start:start + tc, :] * cw_ref[tap:tap + 1, :]
    xl_ref[0:8, :] = xl_ref[tc:tc + 8, :]
    lam = lam_ref[...]
    softplus_neg = jnp.maximum(-lam, 0.0) + jnp.log1p(jnp.exp(-jnp.abs(lam)))
    rows = lax.broadcasted_iota(jnp.int32, (tc // 8, 8, LRU_BLOCK_DIM), 1)
    for blk in range(LRU_BLOCKS):
        lanes = slice(blk * LRU_BLOCK_DIM, (blk + 1) * LRU_BLOCK_DIM)
        xb = xc[:, lanes]
        gates = _dot(xb.astype(BF16), wg_ref[blk])
        r = jax.nn.sigmoid(gates[:, :LRU_BLOCK_DIM] + ba_ref[:, lanes])
        i = jax.nn.sigmoid(gates[:, LRU_BLOCK_DIM:] + bi_ref[:, lanes])
        log_a = -LRU_C * r * softplus_neg[:, lanes]
        a = jnp.exp(log_a)
        mult = jnp.sqrt((1.0 + a * a) * jnp.tanh(-log_a))
        a = a.reshape(tc // 8, 8, LRU_BLOCK_DIM)
        b = (mult * i * xb).reshape(tc // 8, 8, LRU_BLOCK_DIM)
        for d in (1, 2, 4):
            keep = rows >= d
            a_prev = jnp.where(keep, pltpu.roll(a, d, 1), 1.0)
            b_prev = jnp.where(keep, pltpu.roll(b, d, 1), 0.0)
            b = a * b_prev + b
            a = a * a_prev
        carry = hc_ref[:, lanes]
        groups = []
        for j in range(tc // 8):
            hj = a[j] * carry + b[j]
            carry = hj[7:8, :]
            groups.append(hj)
        hc_ref[:, lanes] = carry
        hseq = jnp.concatenate(groups, axis=0)
        lru_lanes = slice(RET_WIDTH + blk * LRU_BLOCK_DIM, RET_WIDTH + (blk + 1) * LRU_BLOCK_DIM)
        merged_ref[:, lru_lanes] = (hseq * jax.nn.gelu(g_lru[:, lanes])).astype(BF16)

    out_ref[...] = x + _dot(merged_ref[...], wout_ref[...])


def _rope_tables(seq):
    half = HEAD_DIM // 2
    inv = ROPE_BASE ** (-jnp.arange(half, dtype=F32) / half)
    ang = jnp.arange(seq, dtype=F32)[:, None] * inv[None, :]
    cos = jnp.cos(ang)
    sin = jnp.sin(ang)
    return jnp.concatenate([cos, cos], axis=-1), jnp.concatenate([-sin, sin], axis=-1)


def _decay_tables(chunk):
    log_gamma = jnp.log1p(-jnp.power(2.0, -5.0 - jnp.arange(RET_HEADS, dtype=F32)))
    pos = jnp.arange(chunk, dtype=F32)
    diff = pos[:, None] - pos[None, :]
    decay = jnp.where(diff >= 0, jnp.exp(log_gamma[:, None, None] * jnp.maximum(diff, 0.0)), 0.0)
    k_decay = jnp.exp(log_gamma[:, None] * (chunk - 1.0 - pos)[None, :])
    q_decay = jnp.exp(log_gamma[:, None] * (pos + 1.0)[None, :])
    chunk_decay = jnp.exp(log_gamma * chunk)

    def lanes(t):
        return jnp.repeat(t.T, HEAD_DIM, axis=1)

    return decay, lanes(q_decay), lanes(k_decay), lanes(chunk_decay[:, None])


def _even_mixer(x, batch, seq, g_mix, w_in, w_out, ret_g, conv_w, conv_b, w_gates, b_a, b_i, lam):
    tc = MIX_ROWS
    tiles = seq // tc
    cos, sin = _rope_tables(seq)
    decay, q_decay, k_decay, chunk_decay = _decay_tables(tc)
    row_spec = pl.BlockSpec((tc, D_MODEL), lambda b, j: (b * tiles + j, 0))
    pos_spec = pl.BlockSpec((tc, HEAD_DIM), lambda b, j: (j, 0))
    return pl.pallas_call(
        _even_body,
        grid=(batch, tiles),
        in_specs=[row_spec, _const_spec((1, D_MODEL)), _const_spec((D_MODEL, IN_EVEN_WIDTH)),
                  _const_spec((D_MODEL, D_MODEL)), _const_spec((1, RET_WIDTH)),
                  _const_spec((CONV_WIDTH, LRU_WIDTH)), _const_spec((1, LRU_WIDTH)),
                  _const_spec((LRU_BLOCKS, LRU_BLOCK_DIM, 2 * LRU_BLOCK_DIM)),
                  _const_spec((1, LRU_WIDTH)), _const_spec((1, LRU_WIDTH)), _const_spec((1, LRU_WIDTH)),
                  pos_spec, pos_spec, _const_spec((RET_HEADS, tc, tc)),
                  _const_spec((tc, RET_WIDTH)), _const_spec((tc, RET_WIDTH)), _const_spec((1, RET_WIDTH))],
        out_specs=row_spec,
        out_shape=jax.ShapeDtypeStruct(x.shape, F32),
        scratch_shapes=[pltpu.VMEM((RET_HEADS, HEAD_DIM, HEAD_DIM), F32),
                        pltpu.VMEM((1, LRU_WIDTH), F32),
                        pltpu.VMEM((tc + 8, LRU_WIDTH), F32),
                        pltpu.VMEM((tc, D_MODEL), BF16)],
        compiler_params=pltpu.CompilerParams(dimension_semantics=("parallel", "arbitrary"),
                                             vmem_limit_bytes=VMEM_LIMIT_BYTES),
        name="even_mixer",
    )(x, g_mix, w_in, w_out, ret_g, conv_w, conv_b, w_gates, b_a, b_i, lam,
      cos, sin, decay, q_decay, k_decay, chunk_decay)


def _s5_discretize(lam_re, lam_im, log_dt):
    dt = jnp.exp(log_dt)
    mag = jnp.exp(lam_re * dt)
    lbar_re = mag * jnp.cos(lam_im * dt)
    lbar_im = mag * jnp.sin(lam_im * dt)
    den = lam_re * lam_re + lam_im * lam_im
    nr = lbar_re - 1.0
    ni = lbar_im
    f_re = (nr * lam_re + ni * lam_im) / den
    f_im = (ni * lam_re - nr * lam_im) / den
    return lbar_re, lbar_im, f_re, f_im


def _s5_ops_body(lre_row_ref, lim_row_ref, ldt_row_ref, lre_col_ref, lim_col_ref, ldt_col_ref,
                 bt_re_ref, bt_im_ref, ct_re_ref, ct_im_ref,
                 a_re_ref, a_im_ref, k_ref, mo_re_ref, mo_im_ref, l8_re_ref, l8_im_ref):
    lr, li, f_re, f_im = _s5_discretize(lre_row_ref[...], lim_row_ref[...], ldt_row_ref[...])
    bt_re = bt_re_ref[...]
    bt_im = bt_im_ref[...]
    bbar_re = f_re * bt_re - f_im * bt_im
    bbar_im = f_re * bt_im + f_im * bt_re
    ct_re = ct_re_ref[...]
    ct_im = ct_im_ref[...]
    pr = jnp.ones_like(lr)
    pi = jnp.zeros_like(li)
    for tau in range(S5_STEP):
        a_re = bbar_re * pr - bbar_im * pi
        a_im = bbar_re * pi + bbar_im * pr
        a_re_ref[:, tau] = a_re
        a_im_ref[:, tau] = a_im
        for gi in range(S5_OPS_GROUPS):
            k_ref[gi, tau] = (
                jnp.dot(a_re[gi], ct_re[gi], precision=lax.Precision.HIGHEST, preferred_element_type=F32)
                - jnp.dot(a_im[gi], ct_im[gi], precision=lax.Precision.HIGHEST, preferred_element_type=F32))
        pr, pi = pr * lr - pi * li, pr * li + pi * lr
    l8_re_ref[...] = pr
    l8_im_ref[...] = pi

    lr_c, li_c, _, _ = _s5_discretize(lre_col_ref[...], lim_col_ref[...], ldt_col_ref[...])
    qr = lr_c
    qi = li_c
    for t in range(S5_STEP):
        mo_re_ref[:, t] = ct_re * qr - ct_im * qi
        mo_im_ref[:, t] = -(ct_re * qi) - ct_im * qr
        qr, qi = qr * lr_c - qi * li_c, qr * li_c + qi * lr_c


def _s5_operators(lam_re, lam_im, log_dt, b_re, b_im, c_re, c_im):
    g, p, c = S5_GROUPS, S5_STATE, S5_GROUP
    ldt = jnp.broadcast_to(log_dt[:, None], (g, p))
    rows = [a.reshape(g, 1, p) for a in (lam_re, lam_im, ldt)]
    cols = [a.reshape(g, p, 1) for a in (lam_re, lam_im, ldt)]
    bt = [jnp.swapaxes(a, 1, 2) for a in (b_re, b_im)]
    ct = [jnp.swapaxes(a, 1, 2) for a in (c_re, c_im)]

    def spec(*shape):
        nd = len(shape)
        return pl.BlockSpec((S5_OPS_GROUPS,) + shape, lambda i: (i,) + (0,) * nd)

    a_re, a_im, kt, mo_re, mo_im, l8_re, l8_im = pl.pallas_call(
        _s5_ops_body,
        grid=(g // S5_OPS_GROUPS,),
        in_specs=[spec(1, p)] * 3 + [spec(p, 1)] * 3 + [spec(c, p)] * 2 + [spec(p, c)] * 2,
        out_specs=[spec(S5_STEP, c, p), spec(S5_STEP, c, p), spec(S5_STEP, c, c),
                   spec(S5_STEP, p, c), spec(S5_STEP, p, c), spec(1, p), spec(1, p)],
        out_shape=[jax.ShapeDtypeStruct((g, S5_STEP, c, p), F32), jax.ShapeDtypeStruct((g, S5_STEP, c, p), F32),
                   jax.ShapeDtypeStruct((g, S5_STEP, c, c), F32),
                   jax.ShapeDtypeStruct((g, S5_STEP, p, c), F32), jax.ShapeDtypeStruct((g, S5_STEP, p, c), F32),
                   jax.ShapeDtypeStruct((g, 1, p), F32), jax.ShapeDtypeStruct((g, 1, p), F32)],
        compiler_params=pltpu.CompilerParams(dimension_semantics=("parallel",)),
        name="s5_operators",
    )(*rows, *cols, *bt, *ct)

    n = S5_STEP * c
    m_in = jnp.concatenate([a_re[:, ::-1], a_im[:, ::-1]], axis=-1).reshape(g, n, 2 * p)
    lag = jnp.arange(S5_STEP)[None, :] - jnp.arange(S5_STEP)[:, None]
    blocks = jnp.where((lag >= 0)[None, :, :, None, None], kt[:, jnp.maximum(lag, 0)], 0.0)
    m_intra = blocks.transpose(0, 1, 3, 2, 4).reshape(g, n, n)
    m_out = jnp.concatenate([mo_re.transpose(0, 2, 1, 3).reshape(g, p, n),
                             mo_im.transpose(0, 2, 1, 3).reshape(g, p, n)], axis=1)

    def pair(m, split_rows, split_cols):
        m = m.reshape(g // 2, 2, n, n)
        z = jnp.zeros_like(m[:, 0])
        full = jnp.concatenate([jnp.concatenate([m[:, 0], z], axis=2),
                                jnp.concatenate([z, m[:, 1]], axis=2)], axis=1)
        order = jnp.concatenate([jnp.arange(p), 2 * p + jnp.arange(p), p + jnp.arange(p), 3 * p + jnp.arange(p)])
        if split_rows:
            full = full[:, order, :]
        if split_cols:
            full = full[:, :, order]
        return full.astype(BF16)

    l8 = jnp.concatenate([l8_re.reshape(g // 2, 1, 2 * p), l8_im.reshape(g // 2, 1, 2 * p)], axis=-1)
    return pair(m_in, False, True), pair(m_intra, False, False), pair(m_out, True, False), l8


def _s5_body(batch, u_ref, min_ref, mintra_ref, mout_ref, l8_ref, y_ref, s_ref, h_ref):
    rows = u_ref.shape[0]
    half = S5_PAIR // 2
    pairs = S5_LANES // S5_PAIR
    for pp in range(pairs):
        lanes = slice(pp * S5_PAIR, (pp + 1) * S5_PAIR)
        s_ref[:, lanes] = _dot(u_ref[:, lanes], min_ref[pp])

    l8 = [(l8_ref[pp, :, :half], l8_ref[pp, :, half:]) for pp in range(pairs)]

    def step(n, carry):
        r0 = pl.multiple_of(n * batch, batch)
        new = []
        for pp in range(pairs):
            st_re, st_im = carry[pp]
            re_lanes = slice(pp * S5_PAIR, pp * S5_PAIR + half)
            im_lanes = slice(pp * S5_PAIR + half, (pp + 1) * S5_PAIR)
            h_ref[pl.ds(r0, batch), re_lanes] = st_re.astype(BF16)
            h_ref[pl.ds(r0, batch), im_lanes] = st_im.astype(BF16)
            lr, li = l8[pp]
            new.append((lr * st_re - li * st_im + s_ref[pl.ds(r0, batch), re_lanes],
                        lr * st_im + li * st_re + s_ref[pl.ds(r0, batch), im_lanes]))
        return tuple(new)

    zero = jnp.zeros((batch, half), F32)
    lax.fori_loop(0, rows // batch, step, tuple((zero, zero) for _ in range(pairs)), unroll=S5_SCAN_UNROLL)

    for pp in range(pairs):
        lanes = slice(pp * S5_PAIR, (pp + 1) * S5_PAIR)
        y_ref[:, lanes] = _dot(u_ref[:, lanes], mintra_ref[pp]) + _dot(h_ref[:, lanes], mout_ref[pp])


def _s5_scan(u, batch, m_in, m_intra, m_out, l8):
    rows, width = u.shape
    pairs = S5_LANES // S5_PAIR
    lane_spec = pl.BlockSpec((rows, S5_LANES), lambda i: (0, i))
    op_spec = pl.BlockSpec((pairs, S5_PAIR, S5_PAIR), lambda i: (i, 0, 0))
    return pl.pallas_call(
        functools.partial(_s5_body, batch),
        grid=(width // S5_LANES,),
        in_specs=[lane_spec, op_spec, op_spec, op_spec, pl.BlockSpec((pairs, 1, S5_PAIR), lambda i: (i, 0, 0))],
        out_specs=lane_spec,
        out_shape=jax.ShapeDtypeStruct((rows, width), F32),
        scratch_shapes=[pltpu.VMEM((rows, S5_LANES), F32), pltpu.VMEM((rows, S5_LANES), BF16)],
        compiler_params=pltpu.CompilerParams(dimension_semantics=("parallel",),
                                             vmem_limit_bytes=VMEM_LIMIT_BYTES),
        name="s5_scan",
    )(u, m_in, m_intra, m_out, l8)


def _glu_body(x_ref, ys_ref, gmix_ref, d_ref, wab_ref, out_ref, scr_ref):
    batch, steps = x_ref.shape[0], x_ref.shape[1]
    pitch = steps + PITCH_PAD
    x = x_ref[...].reshape(-1, D_MODEL)
    slabs = []
    for lb in range(D_MODEL // 128):
        for c in range(steps // S5_STEP):
            groups = [ys_ref[c, :, (lb * 8 + g8) * 128:(lb * 8 + g8 + 1) * 128] for g8 in range(8)]
            for s, blk in enumerate(_block_transpose(groups)):
                scr_ref[lb, pl.ds(c * S5_STEP + s, batch, stride=pitch), :] = blk
        slabs.append(jnp.concatenate([scr_ref[lb, b * pitch:b * pitch + steps, :] for b in range(batch)], axis=0))
    ys = jnp.concatenate(slabs, axis=1)
    u = _rms(x, gmix_ref[...])
    y = ys + d_ref[...] * u
    z = jax.nn.gelu(y).astype(BF16)
    ab = _dot(z, wab_ref[...])
    out_ref[...] = (x + ab[:, :D_MODEL] * jax.nn.sigmoid(ab[:, D_MODEL:])).reshape(out_ref.shape)


def _glu(x, ys, g_mix, d, w_ab):
    batch, seq, _ = x.shape
    steps = GLU_ROWS // batch
    row_spec = pl.BlockSpec((batch, steps, D_MODEL), lambda i: (0, i, 0))
    ys_spec = pl.BlockSpec((steps // S5_STEP, batch, S5_STEP * D_MODEL), lambda i: (i, 0, 0))
    return pl.pallas_call(
        _glu_body,
        grid=(seq // steps,),
        in_specs=[row_spec, ys_spec, _const_spec((1, D_MODEL)), _const_spec((1, D_MODEL)),
                  _const_spec((D_MODEL, 2 * D_MODEL))],
        out_specs=row_spec,
        out_shape=jax.ShapeDtypeStruct(x.shape, F32),
        scratch_shapes=[pltpu.VMEM((D_MODEL // 128, batch * (steps + PITCH_PAD), 128), F32)],
        compiler_params=pltpu.CompilerParams(dimension_semantics=("parallel",),
                                             vmem_limit_bytes=VMEM_LIMIT_BYTES),
        name="glu",
    )(x, ys, g_mix, d, w_ab)


def kernel(x, ffn_norm_g, ffn_w1, ffn_w3, ffn_w2, mix_norm_g, w_in_even, w_out_even, ret_norm_g, conv_w, conv_b, lru_w_a, lru_b_a, lru_w_i, lru_b_i, lru_lambda, s5_lambda_re, s5_lambda_im, s5_log_dt, s5_b_re, s5_b_im, s5_c_re, s5_c_im, s5_d, glu_w_a, glu_w_b, final_norm_g):
    batch, seq, _ = x.shape
    t = batch * seq
    chunk_steps = FFN_ROWS // batch
    assert FFN_ROWS == GLU_ROWS and FFN_ROWS % batch == 0 and batch % 8 == 0
    assert seq % MIX_ROWS == 0 and t % FFN_ROWS == 0 and chunk_steps % S5_STEP == 0 and seq % chunk_steps == 0
    assert S5_GROUPS % S5_OPS_GROUPS == 0
    row = lambda v: v.reshape(1, -1).astype(F32)
    w1, w3, w2 = ffn_w1.astype(BF16), ffn_w3.astype(BF16), ffn_w2.astype(BF16)

    def ffn(xv, layer, half, g2, post):
        return _ffn(xv, row(ffn_norm_g[layer, half]), w1, w3, w2, row(g2), post, layer, half)

    (xf,) = ffn(x.reshape(t, D_MODEL), 0, 0, final_norm_g, "plain")
    w_gates = jnp.concatenate([lru_w_a[0], lru_w_i[0]], axis=-1).astype(BF16)
    xf = _even_mixer(xf, batch, seq, row(mix_norm_g[0]), w_in_even[0].astype(BF16), w_out_even[0].astype(BF16),
                     row(ret_norm_g[0]), conv_w[0].astype(F32), row(conv_b[0]), w_gates,
                     row(lru_b_a[0]), row(lru_b_i[0]), row(lru_lambda[0]))
    (xf,) = ffn(xf, 0, 1, final_norm_g, "plain")

    x3, u = ffn(xf.reshape(batch, seq, D_MODEL), 1, 0, mix_norm_g[1], "chunked")
    chunks = seq // S5_STEP
    m_in, m_intra, m_out, l8 = _s5_operators(s5_lambda_re[0], s5_lambda_im[0], s5_log_dt[0],
                                             s5_b_re[0], s5_b_im[0], s5_c_re[0], s5_c_im[0])
    ys = _s5_scan(u.reshape(chunks * batch, S5_STEP * D_MODEL), batch, m_in, m_intra, m_out, l8)
    w_ab = jnp.concatenate([glu_w_a[0], glu_w_b[0]], axis=-1).astype(BF16)
    x3 = _glu(x3, ys.reshape(chunks, batch, S5_STEP * D_MODEL), row(mix_norm_g[1]), row(s5_d[0]), w_ab)
    (out,) = ffn(x3.reshape(t, D_MODEL), 1, 1, final_norm_g, "final")
    return out.reshape(batch, seq, D_MODEL)
```

```python
import functools
import math

import jax
import jax.numpy as jnp
from jax import lax
from jax.experimental import pallas as pl
from jax.experimental.pallas import tpu as pltpu

F32 = jnp.float32
BF16 = jnp.bfloat16

D_MODEL = 1024
D_FF = 2816
EPS = 1e-6
RET_HEADS = 4
HEAD_DIM = 128
RET_WIDTH = RET_HEADS * HEAD_DIM
ROPE_BASE = 10000.0
LRU_WIDTH = 512
LRU_BLOCKS = 4
LRU_BLOCK_DIM = 128
CONV_WIDTH = 4
LRU_C = 8.0
IN_EVEN_WIDTH = 4 * RET_WIDTH + 2 * LRU_WIDTH
S5_GROUP = 16
S5_GROUPS = 64
S5_STATE = 64
S5_STEP = 8
S5_PAIR = 2 * S5_STEP * S5_GROUP

VMEM_LIMIT_BYTES = 56 * 1024 * 1024

FFN_ROWS = 512
FFN_COLS = 256
MIX_ROWS = 256
GLU_ROWS = 512
S5_LANES = 512
S5_OPS_GROUPS = 8
S5_SCAN_UNROLL = 8
PITCH_PAD = 8


def _rms(x, g):
    return x * lax.rsqrt(jnp.mean(x * x, axis=-1, keepdims=True) + EPS) * g


def _dot(a, b):
    return jnp.dot(a, b, preferred_element_type=F32)


def _const_spec(shape):
    nd = len(shape)
    return pl.BlockSpec(shape, lambda *_: (0,) * nd, pipeline_mode=pl.Buffered(1))


def _block_transpose(v):
    lane_block = lax.broadcasted_iota(jnp.int32, v[0].shape, 1) // S5_GROUP
    for k in (4, 2, 1):
        upper = (lane_block & k) != 0
        new = list(v)
        for a in range(8):
            if a & k == 0:
                new[a] = jnp.where(upper, pltpu.roll(v[a + k], S5_GROUP * k, 1), v[a])
                new[a + k] = jnp.where(upper, v[a + k], pltpu.roll(v[a], 128 - S5_GROUP * k, 1))
        v = new
    return v


def _ffn_body(post, x_ref, g_ref, w1_ref, w3_ref, w2_ref, g2_ref, *out_refs):
    x = x_ref[...].reshape(-1, D_MODEL)
    lane_blocks = D_MODEL // 128
    if post == "chunked":
        u_ref, scr_ref = out_refs[1], out_refs[2]
        batch, steps = x_ref.shape[0], x_ref.shape[1]
        pitch = steps + PITCH_PAD

        @pl.when(pl.program_id(0) == 0)
        def _():
            scr_ref[...] = jnp.zeros(scr_ref.shape, F32)

        def relayout_previous(lb):
            for c in range(steps // S5_STEP):
                per_step = [scr_ref[lb, pl.ds(c * S5_STEP + s, batch, stride=pitch), :]
                            for s in range(S5_STEP)]
                for g8, blk in enumerate(_block_transpose(per_step)):
                    lane0 = (lb * 8 + g8) * 128
                    u_ref[c, :, lane0:lane0 + 128] = blk.astype(BF16)

    xn = _rms(x, g_ref[...]).astype(BF16)
    acc = jnp.zeros(x.shape, F32)
    for ci, f in enumerate(range(0, D_FF, FFN_COLS)):
        h1 = _dot(xn, w1_ref[:, f:f + FFN_COLS])
        h3 = _dot(xn, w3_ref[:, f:f + FFN_COLS])
        if post == "chunked" and ci < lane_blocks:
            relayout_previous(ci)
        gate = (jax.nn.silu(h1) * h3).astype(BF16)
        acc = acc + _dot(gate, w2_ref[f:f + FFN_COLS, :])
    y = x + 0.5 * acc
    out_refs[0][...] = y.reshape(out_refs[0].shape)
    if post == "chunked":
        hn = _rms(y, g2_ref[...])
        for lb in range(lane_blocks):
            for b in range(batch):
                scr_ref[lb, b * pitch:b * pitch + steps, :] = hn[b * steps:(b + 1) * steps, lb * 128:(lb + 1) * 128]


def _ffn(x, g, w1, w3, w2, g2, post, layer, half):
    scratch = []
    if post == "chunked":
        batch, seq, _ = x.shape
        steps = FFN_ROWS // batch
        tiles = seq // steps
        grid = tiles + 1
        semantics = "arbitrary"
        row_spec = pl.BlockSpec((batch, steps, D_MODEL), lambda i: (0, jnp.minimum(i, tiles - 1), 0))
        out_shape = [jax.ShapeDtypeStruct(x.shape, F32),
                     jax.ShapeDtypeStruct((seq // S5_STEP, batch, S5_STEP * D_MODEL), BF16)]
        out_specs = [row_spec, pl.BlockSpec((steps // S5_STEP, batch, S5_STEP * D_MODEL),
                                            lambda i: (jnp.maximum(i - 1, 0), 0, 0))]
        scratch.append(pltpu.VMEM((D_MODEL // 128, batch * (steps + PITCH_PAD), 128), F32))
    else:
        grid = x.shape[0] // FFN_ROWS
        semantics = "parallel"
        row_spec = pl.BlockSpec((FFN_ROWS, D_MODEL), lambda i: (i, 0))
        out_shape = [jax.ShapeDtypeStruct(x.shape, F32)]
        out_specs = [row_spec]

    def weight_spec(rows, cols):
        return pl.BlockSpec((None, None, rows, cols), lambda i: (layer, half, 0, 0), pipeline_mode=pl.Buffered(1))

    return pl.pallas_call(
        functools.partial(_ffn_body, post),
        grid=(grid,),
        in_specs=[row_spec, _const_spec((1, D_MODEL)), weight_spec(D_MODEL, D_FF), weight_spec(D_MODEL, D_FF),
                  weight_spec(D_FF, D_MODEL), _const_spec((1, D_MODEL))],
        out_specs=out_specs,
        out_shape=out_shape,
        scratch_shapes=scratch,
        compiler_params=pltpu.CompilerParams(dimension_semantics=(semantics,),
                                             vmem_limit_bytes=VMEM_LIMIT_BYTES),
        name="ffn_" + post,
    )(x, g, w1, w3, w2, g2)


def _even_body(x_ref, gmix_ref, win_ref, wout_ref, rg_ref, cw_ref, cb_ref, wg_ref, ba_ref, bi_ref,
               lam_ref, cos_ref, sin_ref, dec_ref, qdec_ref, kdec_ref, cdec_ref,
               out_ref, r_ref, hc_ref, xl_ref, merged_ref):
    tc = x_ref.shape[0]

    @pl.when(pl.program_id(1) == 0)
    def _():
        r_ref[...] = jnp.zeros(r_ref.shape, F32)
        hc_ref[...] = jnp.zeros(hc_ref.shape, F32)
        xl_ref[0:8, :] = jnp.zeros((8, LRU_WIDTH), F32)

    x = x_ref[...]
    h = _rms(x, gmix_ref[...]).astype(BF16)
    proj = _dot(h, win_ref[...])

    cos = cos_ref[...]
    sin = sin_ref[...]
    for hd in range(RET_HEADS):
        lanes = slice(hd * HEAD_DIM, (hd + 1) * HEAD_DIM)
        q = proj[:, hd * HEAD_DIM:(hd + 1) * HEAD_DIM]
        k = proj[:, RET_WIDTH + hd * HEAD_DIM:RET_WIDTH + (hd + 1) * HEAD_DIM]
        v = proj[:, 2 * RET_WIDTH + hd * HEAD_DIM:2 * RET_WIDTH + (hd + 1) * HEAD_DIM]
        g_ret = proj[:, 3 * RET_WIDTH + hd * HEAD_DIM:3 * RET_WIDTH + (hd + 1) * HEAD_DIM]
        q = q * cos + pltpu.roll(q, HEAD_DIM // 2, 1) * sin
        k = (k * cos + pltpu.roll(k, HEAD_DIM // 2, 1) * sin) * (HEAD_DIM ** -0.5)
        vb = v.astype(BF16)
        scores = lax.dot_general(q.astype(BF16), k.astype(BF16), (((1,), (1,)), ((), ())),
                                 preferred_element_type=F32) * dec_ref[hd]
        state = r_ref[hd]
        ret = _dot(scores.astype(BF16), vb) + _dot((q * qdec_ref[:, lanes]).astype(BF16), state.astype(BF16))
        kv = lax.dot_general((k * kdec_ref[:, lanes]).astype(BF16), vb, (((0,), (0,)), ((), ())),
                             preferred_element_type=F32)
        r_ref[hd] = state * cdec_ref[:, lanes] + kv
        mu = jnp.mean(ret, axis=-1, keepdims=True)
        cen = ret - mu
        var = jnp.mean(cen * cen, axis=-1, keepdims=True)
        normed = cen * lax.rsqrt(var + EPS) * rg_ref[:, lanes]
        merged_ref[:, lanes] = (normed * jax.nn.silu(g_ret)).astype(BF16)

    x_lru = proj[:, 4 * RET_WIDTH:4 * RET_WIDTH + LRU_WIDTH]
    g_lru = proj[:, 4 * RET_WIDTH + LRU_WIDTH:]
    xl_ref[8:8 + tc, :] = x_lru
    xc = cb_ref[...]
    for tap in range(CONV_WIDTH):
        start = 8 - (CONV_WIDTH - 1) + tap
        xc = xc + xl_ref[start:start + tc, :] * cw_ref[tap:tap + 1, :]
    xl_ref[0:8, :] = xl_ref[tc:tc + 8, :]
    lam = lam_ref[...]
    softplus_neg = jnp.maximum(-lam, 0.0) + jnp.log1p(jnp.exp(-jnp.abs(lam)))
    rows = lax.broadcasted_iota(jnp.int32, (tc // 8, 8, LRU_BLOCK_DIM), 1)
    for blk in range(LRU_BLOCKS):
        lanes = slice(blk * LRU_BLOCK_DIM, (blk + 1) * LRU_BLOCK_DIM)
        xb = xc[:, lanes]
        gates = _dot(xb.astype(BF16), wg_ref[blk])
        r = jax.nn.sigmoid(gates[:, :LRU_BLOCK_DIM] + ba_ref[:, lanes])
        i = jax.nn.sigmoid(gates[:, LRU_BLOCK_DIM:] + bi_ref[:, lanes])
        log_a = -LRU_C * r * softplus_neg[:, lanes]
        a = jnp.exp(log_a)
        mult = jnp.sqrt((1.0 + a * a) * jnp.tanh(-log_a))
        a = a.reshape(tc // 8, 8, LRU_BLOCK_DIM)
        b = (mult * i * xb).reshape(tc // 8, 8, LRU_BLOCK_DIM)
        for d in (1, 2, 4):
            keep = rows >= d
            a_prev = jnp.where(keep, pltpu.roll(a, d, 1), 1.0)
            b_prev = jnp.where(keep, pltpu.roll(b, d, 1), 0.0)
            b = a * b_prev + b
            a = a * a_prev
        carry = hc_ref[:, lanes]
        groups = []
        for j in range(tc // 8):
            hj = a[j] * carry + b[j]
            carry = hj[7:8, :]
            groups.append(hj)
        hc_ref[:, lanes] = carry
        hseq = jnp.concatenate(groups, axis=0)
        lru_lanes = slice(RET_WIDTH + blk * LRU_BLOCK_DIM, RET_WIDTH + (blk + 1) * LRU_BLOCK_DIM)
        merged_ref[:, lru_lanes] = (hseq * jax.nn.gelu(g_lru[:, lanes])).astype(BF16)

    out_ref[...] = x + _dot(merged_ref[...], wout_ref[...])


def _rope_tables(seq):
    half = HEAD_DIM // 2
    inv = ROPE_BASE ** (-jnp.arange(half, dtype=F32) / half)
    ang = jnp.arange(seq, dtype=F32)[:, None] * inv[None, :]
    cos = jnp.cos(ang)
    sin = jnp.sin(ang)
    return jnp.concatenate([cos, cos], axis=-1), jnp.concatenate([-sin, sin], axis=-1)


def _decay_tables(chunk):
    log_gamma = jnp.log1p(-jnp.power(2.0, -5.0 - jnp.arange(RET_HEADS, dtype=F32)))
    pos = jnp.arange(chunk, dtype=F32)
    diff = pos[:, None] - pos[None, :]
    decay = jnp.where(diff >= 0, jnp.exp(log_gamma[:, None, None] * jnp.maximum(diff, 0.0)), 0.0)
    k_decay = jnp.exp(log_gamma[:, None] * (chunk - 1.0 - pos)[None, :])
    q_decay = jnp.exp(log_gamma[:, None] * (pos + 1.0)[None, :])
    chunk_decay = jnp.exp(log_gamma * chunk)

    def lanes(t):
        return jnp.repeat(t.T, HEAD_DIM, axis=1)

    return decay, lanes(q_decay), lanes(k_decay), lanes(chunk_decay[:, None])


def _even_mixer(x, batch, seq, g_mix, w_in, w_out, ret_g, conv_w, conv_b, w_gates, b_a, b_i, lam):
    tc = MIX_ROWS
    tiles = seq // tc
    cos, sin = _rope_tables(seq)
    decay, q_decay, k_decay, chunk_decay = _decay_tables(tc)
    row_spec = pl.BlockSpec((tc, D_MODEL), lambda b, j: (b * tiles + j, 0))
    pos_spec = pl.BlockSpec((tc, HEAD_DIM), lambda b, j: (j, 0))
    return pl.pallas_call(
        _even_body,
        grid=(batch, tiles),
        in_specs=[row_spec, _const_spec((1, D_MODEL)), _const_spec((D_MODEL, IN_EVEN_WIDTH)),
                  _const_spec((D_MODEL, D_MODEL)), _const_spec((1, RET_WIDTH)),
                  _const_spec((CONV_WIDTH, LRU_WIDTH)), _const_spec((1, LRU_WIDTH)),
                  _const_spec((LRU_BLOCKS, LRU_BLOCK_DIM, 2 * LRU_BLOCK_DIM)),
                  _const_spec((1, LRU_WIDTH)), _const_spec((1, LRU_WIDTH)), _const_spec((1, LRU_WIDTH)),
                  pos_spec, pos_spec, _const_spec((RET_HEADS, tc, tc)),
                  _const_spec((tc, RET_WIDTH)), _const_spec((tc, RET_WIDTH)), _const_spec((1, RET_WIDTH))],
        out_specs=row_spec,
        out_shape=jax.ShapeDtypeStruct(x.shape, F32),
        scratch_shapes=[pltpu.VMEM((RET_HEADS, HEAD_DIM, HEAD_DIM), F32),
                        pltpu.VMEM((1, LRU_WIDTH), F32),
                        pltpu.VMEM((tc + 8, LRU_WIDTH), F32),
                        pltpu.VMEM((tc, D_MODEL), BF16)],
        compiler_params=pltpu.CompilerParams(dimension_semantics=("parallel", "arbitrary"),
                                             vmem_limit_bytes=VMEM_LIMIT_BYTES),
        name="even_mixer",
    )(x, g_mix, w_in, w_out, ret_g, conv_w, conv_b, w_gates, b_a, b_i, lam,
      cos, sin, decay, q_decay, k_decay, chunk_decay)


def _s5_discretize(lam_re, lam_im, log_dt):
    dt = jnp.exp(log_dt)
    mag = jnp.exp(lam_re * dt)
    lbar_re = mag * jnp.cos(lam_im * dt)
    lbar_im = mag * jnp.sin(lam_im * dt)
    den = lam_re * lam_re + lam_im * lam_im
    nr = lbar_re - 1.0
    ni = lbar_im
    f_re = (nr * lam_re + ni * lam_im) / den
    f_im = (ni * lam_re - nr * lam_im) / den
    return lbar_re, lbar_im, f_re, f_im


def _s5_ops_body(lre_row_ref, lim_row_ref, ldt_row_ref, lre_col_ref, lim_col_ref, ldt_col_ref,
                 bt_re_ref, bt_im_ref, ct_re_ref, ct_im_ref,
                 a_re_ref, a_im_ref, k_ref, mo_re_ref, mo_im_ref, l8_re_ref, l8_im_ref):
    lr, li, f_re, f_im = _s5_discretize(lre_row_ref[...], lim_row_ref[...], ldt_row_ref[...])
    bt_re = bt_re_ref[...]
    bt_im = bt_im_ref[...]
    bbar_re = f_re * bt_re - f_im * bt_im
    bbar_im = f_re * bt_im + f_im * bt_re
    ct_re = ct_re_ref[...]
    ct_im = ct_im_ref[...]
    pr = jnp.ones_like(lr)
    pi = jnp.zeros_like(li)
    for tau in range(S5_STEP):
        a_re = bbar_re * pr - bbar_im * pi
        a_im = bbar_re * pi + bbar_im * pr
        a_re_ref[:, tau] = a_re
        a_im_ref[:, tau] = a_im
        for gi in range(S5_OPS_GROUPS):
            k_ref[gi, tau] = (
                jnp.dot(a_re[gi], ct_re[gi], precision=lax.Precision.HIGHEST, preferred_element_type=F32)
                - jnp.dot(a_im[gi], ct_im[gi], precision=lax.Precision.HIGHEST, preferred_element_type=F32))
        pr, pi = pr * lr - pi * li, pr * li + pi * lr
    l8_re_ref[...] = pr
    l8_im_ref[...] = pi

    lr_c, li_c, _, _ = _s5_discretize(lre_col_ref[...], lim_col_ref[...], ldt_col_ref[...])
    qr = lr_c
    qi = li_c
    for t in range(S5_STEP):
        mo_re_ref[:, t] = ct_re * qr - ct_im * qi
        mo_im_ref[:, t] = -(ct_re * qi) - ct_im * qr
        qr, qi = qr * lr_c - qi * li_c, qr * li_c + qi * lr_c


def _s5_operators(lam_re, lam_im, log_dt, b_re, b_im, c_re, c_im):
    g, p, c = S5_GROUPS, S5_STATE, S5_GROUP
    ldt = jnp.broadcast_to(log_dt[:, None], (g, p))
    rows = [a.reshape(g, 1, p) for a in (lam_re, lam_im, ldt)]
    cols = [a.reshape(g, p, 1) for a in (lam_re, lam_im, ldt)]
    bt = [jnp.swapaxes(a, 1, 2) for a in (b_re, b_im)]
    ct = [jnp.swapaxes(a, 1, 2) for a in (c_re, c_im)]

    def spec(*shape):
        nd = len(shape)
        return pl.BlockSpec((S5_OPS_GROUPS,) + shape, lambda i: (i,) + (0,) * nd)

    a_re, a_im, kt, mo_re, mo_im, l8_re, l8_im = pl.pallas_call(
        _s5_ops_body,
        grid=(g // S5_OPS_GROUPS,),
        in_specs=[spec(1, p)] * 3 + [spec(p, 1)] * 3 + [spec(c, p)] * 2 + [spec(p, c)] * 2,
        out_specs=[spec(S5_STEP, c, p), spec(S5_STEP, c, p), spec(S5_STEP, c, c),
                   spec(S5_STEP, p, c), spec(S5_STEP, p, c), spec(1, p), spec(1, p)],
        out_shape=[jax.ShapeDtypeStruct((g, S5_STEP, c, p), F32), jax.ShapeDtypeStruct((g, S5_STEP, c, p), F32),
                   jax.ShapeDtypeStruct((g, S5_STEP, c, c), F32),
                   jax.ShapeDtypeStruct((g, S5_STEP, p, c), F32), jax.ShapeDtypeStruct((g, S5_STEP, p, c), F32),
                   jax.ShapeDtypeStruct((g, 1, p), F32), jax.ShapeDtypeStruct((g, 1, p), F32)],
        compiler_params=pltpu.CompilerParams(dimension_semantics=("parallel",)),
        name="s5_operators",
    )(*rows, *cols, *bt, *ct)

    n = S5_STEP * c
    m_in = jnp.concatenate([a_re[:, ::-1], a_im[:, ::-1]], axis=-1).reshape(g, n, 2 * p)
    lag = jnp.arange(S5_STEP)[None, :] - jnp.arange(S5_STEP)[:, None]
    blocks = jnp.where((lag >= 0)[None, :, :, None, None], kt[:, jnp.maximum(lag, 0)], 0.0)
    m_intra = blocks.transpose(0, 1, 3, 2, 4).reshape(g, n, n)
    m_out = jnp.concatenate([mo_re.transpose(0, 2, 1, 3).reshape(g, p, n),
                             mo_im.transpose(0, 2, 1, 3).reshape(g, p, n)], axis=1)

    def pair(m, split_rows, split_cols):
        m = m.reshape(g // 2, 2, n, n)
        z = jnp.zeros_like(m[:, 0])
        full = jnp.concatenate([jnp.concatenate([m[:, 0], z], axis=2),
                                jnp.concatenate([z, m[:, 1]], axis=2)], axis=1)
        order = jnp.concatenate([jnp.arange(p), 2 * p + jnp.arange(p), p + jnp.arange(p), 3 * p + jnp.arange(p)])
        if split_rows:
            full = full[:, order, :]
        if split_cols:
            full = full[:, :, order]
        return full.astype(BF16)

    l8 = jnp.concatenate([l8_re.reshape(g // 2, 1, 2 * p), l8_im.reshape(g // 2, 1, 2 * p)], axis=-1)
    return pair(m_in, False, True), pair(m_intra, False, False), pair(m_out, True, False), l8


def _s5_body(batch, u_ref, min_ref, mintra_ref, mout_ref, l8_ref, y_ref, s_ref, h_ref):
    rows = u_ref.shape[0]
    half = S5_PAIR // 2
    pairs = S5_LANES // S5_PAIR
    for pp in range(pairs):
        lanes = slice(pp * S5_PAIR, (pp + 1) * S5_PAIR)
        s_ref[:, lanes] = _dot(u_ref[:, lanes], min_ref[pp])

    l8 = [(l8_ref[pp, :, :half], l8_ref[pp, :, half:]) for pp in range(pairs)]

    def step(n, carry):
        r0 = pl.multiple_of(n * batch, batch)
        new = []
        for pp in range(pairs):
            st_re, st_im = carry[pp]
            re_lanes = slice(pp * S5_PAIR, pp * S5_PAIR + half)
            im_lanes = slice(pp * S5_PAIR + half, (pp + 1) * S5_PAIR)
            h_ref[pl.ds(r0, batch), re_lanes] = st_re.astype(BF16)
            h_ref[pl.ds(r0, batch), im_lanes] = st_im.astype(BF16)
            lr, li = l8[pp]
            new.append((lr * st_re - li * st_im + s_ref[pl.ds(r0, batch), re_lanes],
                        lr * st_im + li * st_re + s_ref[pl.ds(r0, batch), im_lanes]))
        return tuple(new)

    zero = jnp.zeros((batch, half), F32)
    lax.fori_loop(0, rows // batch, step, tuple((zero, zero) for _ in range(pairs)), unroll=S5_SCAN_UNROLL)

    for pp in range(pairs):
        lanes = slice(pp * S5_PAIR, (pp + 1) * S5_PAIR)
        y_ref[:, lanes] = _dot(u_ref[:, lanes], mintra_ref[pp]) + _dot(h_ref[:, lanes], mout_ref[pp])


def _s5_scan(u, batch, m_in, m_intra, m_out, l8):
    rows, width = u.shape
    pairs = S5_LANES // S5_PAIR
    lane_spec = pl.BlockSpec((rows, S5_LANES), lambda i: (0, i))
    op_spec = pl.BlockSpec((pairs, S5_PAIR, S5_PAIR), lambda i: (i, 0, 0))
    return pl.pallas_call(
        functools.partial(_s5_body, batch),
        grid=(width // S5_LANES,),
        in_specs=[lane_spec, op_spec, op_spec, op_spec, pl.BlockSpec((pairs, 1, S5_PAIR), lambda i: (i, 0, 0))],
        out_specs=lane_spec,
        out_shape=jax.ShapeDtypeStruct((rows, width), F32),
        scratch_shapes=[pltpu.VMEM((rows, S5_LANES), F32), pltpu.VMEM((rows, S5_LANES), BF16)],
        compiler_params=pltpu.CompilerParams(dimension_semantics=("parallel",),
                                             vmem_limit_bytes=VMEM_LIMIT_BYTES),
        name="s5_scan",
    )(u, m_in, m_intra, m_out, l8)


def _glu_ffn_body(x_ref, ys_ref, gmix_ref, d_ref, wab_ref, g_ref, w1_ref, w3_ref, w2_ref, g2_ref, out_ref, scr_ref):
    batch, steps = x_ref.shape[0], x_ref.shape[1]
    pitch = steps + PITCH_PAD
    lane_blocks = D_MODEL // 128

    @pl.when(pl.program_id(0) == 0)
    def _():
        scr_ref[...] = jnp.zeros(scr_ref.shape, F32)

    x = x_ref[...].reshape(-1, D_MODEL)
    ys = jnp.concatenate(
        [jnp.concatenate([scr_ref[lb, b * pitch:b * pitch + steps, :] for b in range(batch)], axis=0)
         for lb in range(lane_blocks)], axis=1)
    y = ys + d_ref[...] * _rms(x, gmix_ref[...])
    ab = _dot(jax.nn.gelu(y).astype(BF16), wab_ref[...])
    x = x + ab[:, :D_MODEL] * jax.nn.sigmoid(ab[:, D_MODEL:])

    def relayout_next(lb):
        for c in range(steps // S5_STEP):
            groups = [ys_ref[c, :, (lb * 8 + g8) * 128:(lb * 8 + g8 + 1) * 128] for g8 in range(8)]
            for s, blk in enumerate(_block_transpose(groups)):
                scr_ref[lb, pl.ds(c * S5_STEP + s, batch, stride=pitch), :] = blk

    xn = _rms(x, g_ref[...]).astype(BF16)
    acc = jnp.zeros(x.shape, F32)
    for ci, f in enumerate(range(0, D_FF, FFN_COLS)):
        h1 = _dot(xn, w1_ref[:, f:f + FFN_COLS])
        h3 = _dot(xn, w3_ref[:, f:f + FFN_COLS])
        if ci < lane_blocks:
            relayout_next(ci)
        gate = (jax.nn.silu(h1) * h3).astype(BF16)
        acc = acc + _dot(gate, w2_ref[f:f + FFN_COLS, :])
    out_ref[...] = _rms(x + 0.5 * acc, g2_ref[...]).reshape(out_ref.shape)


def _glu_ffn(x, ys, g_mix, d, w_ab, g, w1, w3, w2, g2, layer, half):
    batch, seq, _ = x.shape
    steps = GLU_ROWS // batch
    tiles = seq // steps
    row_spec = pl.BlockSpec((batch, steps, D_MODEL), lambda j: (0, jnp.maximum(j - 1, 0), 0))
    ys_spec = pl.BlockSpec((steps // S5_STEP, batch, S5_STEP * D_MODEL), lambda j: (jnp.minimum(j, tiles - 1), 0, 0))

    def weight_spec(rows, cols):
        return pl.BlockSpec((None, None, rows, cols), lambda j: (layer, half, 0, 0), pipeline_mode=pl.Buffered(1))

    return pl.pallas_call(
        _glu_ffn_body,
        grid=(tiles + 1,),
        in_specs=[row_spec, ys_spec, _const_spec((1, D_MODEL)), _const_spec((1, D_MODEL)),
                  _const_spec((D_MODEL, 2 * D_MODEL)), _const_spec((1, D_MODEL)),
                  weight_spec(D_MODEL, D_FF), weight_spec(D_MODEL, D_FF), weight_spec(D_FF, D_MODEL),
                  _const_spec((1, D_MODEL))],
        out_specs=row_spec,
        out_shape=jax.ShapeDtypeStruct(x.shape, F32),
        scratch_shapes=[pltpu.VMEM((D_MODEL // 128, batch * (steps + PITCH_PAD), 128), F32)],
        compiler_params=pltpu.CompilerParams(dimension_semantics=("arbitrary",),
                                             vmem_limit_bytes=VMEM_LIMIT_BYTES),
        name="glu_ffn_final",
    )(x, ys, g_mix, d, w_ab, g, w1, w3, w2, g2)


def kernel(x, ffn_norm_g, ffn_w1, ffn_w3, ffn_w2, mix_norm_g, w_in_even, w_out_even, ret_norm_g, conv_w, conv_b, lru_w_a, lru_b_a, lru_w_i, lru_b_i, lru_lambda, s5_lambda_re, s5_lambda_im, s5_log_dt, s5_b_re, s5_b_im, s5_c_re, s5_c_im, s5_d, glu_w_a, glu_w_b, final_norm_g):
    batch, seq, _ = x.shape
    t = batch * seq
    chunk_steps = FFN_ROWS // batch
    assert FFN_ROWS == GLU_ROWS and FFN_ROWS % batch == 0 and batch % 8 == 0
    assert seq % MIX_ROWS == 0 and t % FFN_ROWS == 0 and chunk_steps % S5_STEP == 0 and seq % chunk_steps == 0
    assert S5_GROUPS % S5_OPS_GROUPS == 0
    row = lambda v: v.reshape(1, -1).astype(F32)
    w1, w3, w2 = ffn_w1.astype(BF16), ffn_w3.astype(BF16), ffn_w2.astype(BF16)

    def ffn(xv, layer, half, g2, post):
        return _ffn(xv, row(ffn_norm_g[layer, half]), w1, w3, w2, row(g2), post, layer, half)

    (xf,) = ffn(x.reshape(t, D_MODEL), 0, 0, final_norm_g, "plain")
    w_gates = jnp.concatenate([lru_w_a[0], lru_w_i[0]], axis=-1).astype(BF16)
    xf = _even_mixer(xf, batch, seq, row(mix_norm_g[0]), w_in_even[0].astype(BF16), w_out_even[0].astype(BF16),
                     row(ret_norm_g[0]), conv_w[0].astype(F32), row(conv_b[0]), w_gates,
                     row(lru_b_a[0]), row(lru_b_i[0]), row(lru_lambda[0]))
    (xf,) = ffn(xf, 0, 1, final_norm_g, "plain")

    x3, u = ffn(xf.reshape(batch, seq, D_MODEL), 1, 0, mix_norm_g[1], "chunked")
    chunks = seq // S5_STEP
    m_in, m_intra, m_out, l8 = _s5_operators(s5_lambda_re[0], s5_lambda_im[0], s5_log_dt[0],
                                             s5_b_re[0], s5_b_im[0], s5_c_re[0], s5_c_im[0])
    ys = _s5_scan(u.reshape(chunks * batch, S5_STEP * D_MODEL), batch, m_in, m_intra, m_out, l8)
    w_ab = jnp.concatenate([glu_w_a[0], glu_w_b[0]], axis=-1).astype(BF16)
    return _glu_ffn(x3, ys.reshape(chunks, batch, S5_STEP * D_MODEL), row(mix_norm_g[1]), row(s5_d[0]), w_ab,
                    row(ffn_norm_g[1, 1]), w1, w3, w2, row(final_norm_g), 1, 1)
```

```python
import functools
import math

import jax
import jax.numpy as jnp
from jax import lax
from jax.experimental import pallas as pl
from jax.experimental.pallas import tpu as pltpu

F32 = jnp.float32
BF16 = jnp.bfloat16

D_MODEL = 1024
D_FF = 2816
EPS = 1e-6
RET_HEADS = 4
HEAD_DIM = 128
RET_WIDTH = RET_HEADS * HEAD_DIM
ROPE_BASE = 10000.0
LRU_WIDTH = 512
LRU_BLOCKS = 4
LRU_BLOCK_DIM = 128
CONV_WIDTH = 4
LRU_C = 8.0
IN_EVEN_WIDTH = 4 * RET_WIDTH + 2 * LRU_WIDTH
S5_GROUP = 16
S5_GROUPS = 64
S5_STATE = 64
S5_STEP = 8
S5_PAIR = 2 * S5_STEP * S5_GROUP

VMEM_LIMIT_BYTES = 56 * 1024 * 1024

FFN_ROWS = 512
PLAIN_FFN_ROWS = 1024
FFN_COLS = 256
MIX_ROWS = 256
GLU_ROWS = 512
S5_LANES = 512
S5_OPS_GROUPS = 8
S5_SCAN_UNROLL = 8
PITCH_PAD = 8


def _rms(x, g):
    return x * lax.rsqrt(jnp.mean(x * x, axis=-1, keepdims=True) + EPS) * g


def _dot(a, b):
    return jnp.dot(a, b, preferred_element_type=F32)


def _const_spec(shape):
    nd = len(shape)
    return pl.BlockSpec(shape, lambda *_: (0,) * nd, pipeline_mode=pl.Buffered(1))


def _block_transpose(v):
    lane_block = lax.broadcasted_iota(jnp.int32, v[0].shape, 1) // S5_GROUP
    for k in (4, 2, 1):
        upper = (lane_block & k) != 0
        new = list(v)
        for a in range(8):
            if a & k == 0:
                new[a] = jnp.where(upper, pltpu.roll(v[a + k], S5_GROUP * k, 1), v[a])
                new[a + k] = jnp.where(upper, v[a + k], pltpu.roll(v[a], 128 - S5_GROUP * k, 1))
        v = new
    return v


def _ffn_body(post, x_ref, g_ref, w1_ref, w3_ref, w2_ref, g2_ref, *out_refs):
    lane_blocks = D_MODEL // 128

    def tile(between_chunks):
        x = x_ref[...].reshape(-1, D_MODEL)
        xn = _rms(x, g_ref[...]).astype(BF16)
        acc = jnp.zeros(x.shape, F32)
        for ci, f in enumerate(range(0, D_FF, FFN_COLS)):
            h1 = _dot(xn, w1_ref[:, f:f + FFN_COLS])
            h3 = _dot(xn, w3_ref[:, f:f + FFN_COLS])
            if between_chunks is not None and ci < lane_blocks:
                between_chunks(ci)
            gate = (jax.nn.silu(h1) * h3).astype(BF16)
            acc = acc + _dot(gate, w2_ref[f:f + FFN_COLS, :])
        y = x + 0.5 * acc
        out_refs[0][...] = y.reshape(out_refs[0].shape)
        return y

    if post == "plain":
        tile(None)
        return

    u_ref, scr_ref = out_refs[1], out_refs[2]
    batch, steps = x_ref.shape[0], x_ref.shape[1]
    pitch = steps + PITCH_PAD
    step = pl.program_id(0)
    last = pl.num_programs(0) - 1

    def relayout_previous(lb):
        for c in range(steps // S5_STEP):
            per_step = [scr_ref[lb, pl.ds(c * S5_STEP + s, batch, stride=pitch), :]
                        for s in range(S5_STEP)]
            for g8, blk in enumerate(_block_transpose(per_step)):
                lane0 = (lb * 8 + g8) * 128
                u_ref[c, :, lane0:lane0 + 128] = blk.astype(BF16)

    @pl.when(step == 0)
    def _():
        scr_ref[...] = jnp.zeros(scr_ref.shape, F32)

    @pl.when(step < last)
    def _():
        hn = _rms(tile(relayout_previous), g2_ref[...])
        for lb in range(lane_blocks):
            for b in range(batch):
                scr_ref[lb, b * pitch:b * pitch + steps, :] = hn[b * steps:(b + 1) * steps, lb * 128:(lb + 1) * 128]

    @pl.when(step == last)
    def _():
        for lb in range(lane_blocks):
            relayout_previous(lb)


def _ffn(x, g, w1, w3, w2, g2, post, layer, half):
    scratch = []
    if post == "chunked":
        batch, seq, _ = x.shape
        steps = FFN_ROWS // batch
        tiles = seq // steps
        grid = tiles + 1
        semantics = "arbitrary"
        row_spec = pl.BlockSpec((batch, steps, D_MODEL), lambda i: (0, jnp.minimum(i, tiles - 1), 0))
        out_shape = [jax.ShapeDtypeStruct(x.shape, F32),
                     jax.ShapeDtypeStruct((seq // S5_STEP, batch, S5_STEP * D_MODEL), BF16)]
        out_specs = [row_spec, pl.BlockSpec((steps // S5_STEP, batch, S5_STEP * D_MODEL),
                                            lambda i: (jnp.maximum(i - 1, 0), 0, 0))]
        scratch.append(pltpu.VMEM((D_MODEL // 128, batch * (steps + PITCH_PAD), 128), F32))
    else:
        grid = x.shape[0] // PLAIN_FFN_ROWS
        semantics = "parallel"
        row_spec = pl.BlockSpec((PLAIN_FFN_ROWS, D_MODEL), lambda i: (i, 0))
        out_shape = [jax.ShapeDtypeStruct(x.shape, F32)]
        out_specs = [row_spec]

    def weight_spec(rows, cols):
        return pl.BlockSpec((None, None, rows, cols), lambda i: (layer, half, 0, 0), pipeline_mode=pl.Buffered(1))

    return pl.pallas_call(
        functools.partial(_ffn_body, post),
        grid=(grid,),
        in_specs=[row_spec, _const_spec((1, D_MODEL)), weight_spec(D_MODEL, D_FF), weight_spec(D_MODEL, D_FF),
                  weight_spec(D_FF, D_MODEL), _const_spec((1, D_MODEL))],
        out_specs=out_specs,
        out_shape=out_shape,
        scratch_shapes=scratch,
        compiler_params=pltpu.CompilerParams(dimension_semantics=(semantics,),
                                             vmem_limit_bytes=VMEM_LIMIT_BYTES),
        name="ffn_" + post,
    )(x, g, w1, w3, w2, g2)


def _even_body(x_ref, gmix_ref, win_ref, wout_ref, rg_ref, cw_ref, cb_ref, wg_ref, ba_ref, bi_ref,
               lam_ref, cos_ref, sin_ref, dec_ref, qdec_ref, kdec_ref, cdec_ref,
               out_ref, r_ref, hc_ref, xl_ref, merged_ref):
    tc = x_ref.shape[0]

    @pl.when(pl.program_id(1) == 0)
    def _():
        r_ref[...] = jnp.zeros(r_ref.shape, F32)
        hc_ref[...] = jnp.zeros(hc_ref.shape, F32)
        xl_ref[0:8, :] = jnp.zeros((8, LRU_WIDTH), F32)

    x = x_ref[...]
    h = _rms(x, gmix_ref[...]).astype(BF16)
    proj = _dot(h, win_ref[...])

    cos = cos_ref[...]
    sin = sin_ref[...]
    for hd in range(RET_HEADS):
        lanes = slice(hd * HEAD_DIM, (hd + 1) * HEAD_DIM)
        q = proj[:, hd * HEAD_DIM:(hd + 1) * HEAD_DIM]
        k = proj[:, RET_WIDTH + hd * HEAD_DIM:RET_WIDTH + (hd + 1) * HEAD_DIM]
        v = proj[:, 2 * RET_WIDTH + hd * HEAD_DIM:2 * RET_WIDTH + (hd + 1) * HEAD_DIM]
        g_ret = proj[:, 3 * RET_WIDTH + hd * HEAD_DIM:3 * RET_WIDTH + (hd + 1) * HEAD_DIM]
        q = q * cos + pltpu.roll(q, HEAD_DIM // 2, 1) * sin
        k = (k * cos + pltpu.roll(k, HEAD_DIM // 2, 1) * sin) * (HEAD_DIM ** -0.5)
        vb = v.astype(BF16)
        scores = lax.dot_general(q.astype(BF16), k.astype(BF16), (((1,), (1,)), ((), ())),
                                 preferred_element_type=F32) * dec_ref[hd]
        state = r_ref[hd]
        ret = _dot(scores.astype(BF16), vb) + _dot((q * qdec_ref[:, lanes]).astype(BF16), state.astype(BF16))
        kv = lax.dot_general((k * kdec_ref[:, lanes]).astype(BF16), vb, (((0,), (0,)), ((), ())),
                             preferred_element_type=F32)
        r_ref[hd] = state * cdec_ref[:, lanes] + kv
        mu = jnp.mean(ret, axis=-1, keepdims=True)
        cen = ret - mu
        var = jnp.mean(cen * cen, axis=-1, keepdims=True)
        normed = cen * lax.rsqrt(var + EPS) * rg_ref[:, lanes]
        merged_ref[:, lanes] = (normed * jax.nn.silu(g_ret)).astype(BF16)

    x_lru = proj[:, 4 * RET_WIDTH:4 * RET_WIDTH + LRU_WIDTH]
    g_lru = proj[:, 4 * RET_WIDTH + LRU_WIDTH:]
    xl_ref[8:8 + tc, :] = x_lru
    xc = cb_ref[...]
    for tap in range(CONV_WIDTH):
        start = 8 - (CONV_WIDTH - 1) + tap
        xc = xc + xl_ref[start:start + tc, :] * cw_ref[tap:tap + 1, :]
    xl_ref[0:8, :] = xl_ref[tc:tc + 8, :]
    lam = lam_ref[...]
    softplus_neg = jnp.maximum(-lam, 0.0) + jnp.log1p(jnp.exp(-jnp.abs(lam)))
    rows = lax.broadcasted_iota(jnp.int32, (tc // 8, 8, LRU_BLOCK_DIM), 1)
    for blk in range(LRU_BLOCKS):
        lanes = slice(blk * LRU_BLOCK_DIM, (blk + 1) * LRU_BLOCK_DIM)
        xb = xc[:, lanes]
        gates = _dot(xb.astype(BF16), wg_ref[blk])
        r = jax.nn.sigmoid(gates[:, :LRU_BLOCK_DIM] + ba_ref[:, lanes])
        i = jax.nn.sigmoid(gates[:, LRU_BLOCK_DIM:] + bi_ref[:, lanes])
        log_a = -LRU_C * r * softplus_neg[:, lanes]
        a = jnp.exp(log_a)
        mult = jnp.sqrt((1.0 + a * a) * jnp.tanh(-log_a))
        a = a.reshape(tc // 8, 8, LRU_BLOCK_DIM)
        b = (mult * i * xb).reshape(tc // 8, 8, LRU_BLOCK_DIM)
        for d in (1, 2, 4):
            keep = rows >= d
            a_prev = jnp.where(keep, pltpu.roll(a, d, 1), 1.0)
            b_prev = jnp.where(keep, pltpu.roll(b, d, 1), 0.0)
            b = a * b_prev + b
            a = a * a_prev
        carry = hc_ref[:, lanes]
        groups = []
        for j in range(tc // 8):
            hj = a[j] * carry + b[j]
            carry = hj[7:8, :]
            groups.append(hj)
        hc_ref[:, lanes] = carry
        hseq = jnp.concatenate(groups, axis=0)
        lru_lanes = slice(RET_WIDTH + blk * LRU_BLOCK_DIM, RET_WIDTH + (blk + 1) * LRU_BLOCK_DIM)
        merged_ref[:, lru_lanes] = (hseq * jax.nn.gelu(g_lru[:, lanes])).astype(BF16)

    out_ref[...] = x + _dot(merged_ref[...], wout_ref[...])


def _rope_tables(seq):
    half = HEAD_DIM // 2
    inv = ROPE_BASE ** (-jnp.arange(half, dtype=F32) / half)
    ang = jnp.arange(seq, dtype=F32)[:, None] * inv[None, :]
    cos = jnp.cos(ang)
    sin = jnp.sin(ang)
    return jnp.concatenate([cos, cos], axis=-1), jnp.concatenate([-sin, sin], axis=-1)


def _decay_tables(chunk):
    log_gamma = jnp.log1p(-jnp.power(2.0, -5.0 - jnp.arange(RET_HEADS, dtype=F32)))
    pos = jnp.arange(chunk, dtype=F32)
    diff = pos[:, None] - pos[None, :]
    decay = jnp.where(diff >= 0, jnp.exp(log_gamma[:, None, None] * jnp.maximum(diff, 0.0)), 0.0)
    k_decay = jnp.exp(log_gamma[:, None] * (chunk - 1.0 - pos)[None, :])
    q_decay = jnp.exp(log_gamma[:, None] * (pos + 1.0)[None, :])
    chunk_decay = jnp.exp(log_gamma * chunk)

    def lanes(t):
        return jnp.repeat(t.T, HEAD_DIM, axis=1)

    return decay, lanes(q_decay), lanes(k_decay), lanes(chunk_decay[:, None])


def _even_mixer(x, batch, seq, g_mix, w_in, w_out, ret_g, conv_w, conv_b, w_gates, b_a, b_i, lam):
    tc = MIX_ROWS
    tiles = seq // tc
    cos, sin = _rope_tables(seq)
    decay, q_decay, k_decay, chunk_decay = _decay_tables(tc)
    row_spec = pl.BlockSpec((tc, D_MODEL), lambda b, j: (b * tiles + j, 0))
    pos_spec = pl.BlockSpec((tc, HEAD_DIM), lambda b, j: (j, 0))
    return pl.pallas_call(
        _even_body,
        grid=(batch, tiles),
        in_specs=[row_spec, _const_spec((1, D_MODEL)), _const_spec((D_MODEL, IN_EVEN_WIDTH)),
                  _const_spec((D_MODEL, D_MODEL)), _const_spec((1, RET_WIDTH)),
                  _const_spec((CONV_WIDTH, LRU_WIDTH)), _const_spec((1, LRU_WIDTH)),
                  _const_spec((LRU_BLOCKS, LRU_BLOCK_DIM, 2 * LRU_BLOCK_DIM)),
                  _const_spec((1, LRU_WIDTH)), _const_spec((1, LRU_WIDTH)), _const_spec((1, LRU_WIDTH)),
                  pos_spec, pos_spec, _const_spec((RET_HEADS, tc, tc)),
                  _const_spec((tc, RET_WIDTH)), _const_spec((tc, RET_WIDTH)), _const_spec((1, RET_WIDTH))],
        out_specs=row_spec,
        out_shape=jax.ShapeDtypeStruct(x.shape, F32),
        scratch_shapes=[pltpu.VMEM((RET_HEADS, HEAD_DIM, HEAD_DIM), F32),
                        pltpu.VMEM((1, LRU_WIDTH), F32),
                        pltpu.VMEM((tc + 8, LRU_WIDTH), F32),
                        pltpu.VMEM((tc, D_MODEL), BF16)],
        compiler_params=pltpu.CompilerParams(dimension_semantics=("parallel", "arbitrary"),
                                             vmem_limit_bytes=VMEM_LIMIT_BYTES),
        name="even_mixer",
    )(x, g_mix, w_in, w_out, ret_g, conv_w, conv_b, w_gates, b_a, b_i, lam,
      cos, sin, decay, q_decay, k_decay, chunk_decay)


def _s5_discretize(lam_re, lam_im, log_dt):
    dt = jnp.exp(log_dt)
    mag = jnp.exp(lam_re * dt)
    lbar_re = mag * jnp.cos(lam_im * dt)
    lbar_im = mag * jnp.sin(lam_im * dt)
    den = lam_re * lam_re + lam_im * lam_im
    nr = lbar_re - 1.0
    ni = lbar_im
    f_re = (nr * lam_re + ni * lam_im) / den
    f_im = (ni * lam_re - nr * lam_im) / den
    return lbar_re, lbar_im, f_re, f_im


def _s5_ops_body(lre_row_ref, lim_row_ref, ldt_row_ref, lre_col_ref, lim_col_ref, ldt_col_ref,
                 bt_re_ref, bt_im_ref, ct_re_ref, ct_im_ref,
                 a_re_ref, a_im_ref, k_ref, mo_re_ref, mo_im_ref, l8_re_ref, l8_im_ref):
    lr, li, f_re, f_im = _s5_discretize(lre_row_ref[...], lim_row_ref[...], ldt_row_ref[...])
    bt_re = bt_re_ref[...]
    bt_im = bt_im_ref[...]
    bbar_re = f_re * bt_re - f_im * bt_im
    bbar_im = f_re * bt_im + f_im * bt_re
    ct_re = ct_re_ref[...]
    ct_im = ct_im_ref[...]
    pr = jnp.ones_like(lr)
    pi = jnp.zeros_like(li)
    for tau in range(S5_STEP):
        a_re = bbar_re * pr - bbar_im * pi
        a_im = bbar_re * pi + bbar_im * pr
        a_re_ref[:, tau] = a_re
        a_im_ref[:, tau] = a_im
        for gi in range(S5_OPS_GROUPS):
            k_ref[gi, tau] = (
                jnp.dot(a_re[gi], ct_re[gi], precision=lax.Precision.HIGHEST, preferred_element_type=F32)
                - jnp.dot(a_im[gi], ct_im[gi], precision=lax.Precision.HIGHEST, preferred_element_type=F32))
        pr, pi = pr * lr - pi * li, pr * li + pi * lr
    l8_re_ref[...] = pr
    l8_im_ref[...] = pi

    lr_c, li_c, _, _ = _s5_discretize(lre_col_ref[...], lim_col_ref[...], ldt_col_ref[...])
    qr = lr_c
    qi = li_c
    for t in range(S5_STEP):
        mo_re_ref[:, t] = ct_re * qr - ct_im * qi
        mo_im_ref[:, t] = -(ct_re * qi) - ct_im * qr
        qr, qi = qr * lr_c - qi * li_c, qr * li_c + qi * lr_c


def _s5_operators(lam_re, lam_im, log_dt, b_re, b_im, c_re, c_im):
    g, p, c = S5_GROUPS, S5_STATE, S5_GROUP
    ldt = jnp.broadcast_to(log_dt[:, None], (g, p))
    rows = [a.reshape(g, 1, p) for a in (lam_re, lam_im, ldt)]
    cols = [a.reshape(g, p, 1) for a in (lam_re, lam_im, ldt)]
    bt = [jnp.swapaxes(a, 1, 2) for a in (b_re, b_im)]
    ct = [jnp.swapaxes(a, 1, 2) for a in (c_re, c_im)]

    def spec(*shape):
        nd = len(shape)
        return pl.BlockSpec((S5_OPS_GROUPS,) + shape, lambda i: (i,) + (0,) * nd)

    a_re, a_im, kt, mo_re, mo_im, l8_re, l8_im = pl.pallas_call(
        _s5_ops_body,
        grid=(g // S5_OPS_GROUPS,),
        in_specs=[spec(1, p)] * 3 + [spec(p, 1)] * 3 + [spec(c, p)] * 2 + [spec(p, c)] * 2,
        out_specs=[spec(S5_STEP, c, p), spec(S5_STEP, c, p), spec(S5_STEP, c, c),
                   spec(S5_STEP, p, c), spec(S5_STEP, p, c), spec(1, p), spec(1, p)],
        out_shape=[jax.ShapeDtypeStruct((g, S5_STEP, c, p), F32), jax.ShapeDtypeStruct((g, S5_STEP, c, p), F32),
                   jax.ShapeDtypeStruct((g, S5_STEP, c, c), F32),
                   jax.ShapeDtypeStruct((g, S5_STEP, p, c), F32), jax.ShapeDtypeStruct((g, S5_STEP, p, c), F32),
                   jax.ShapeDtypeStruct((g, 1, p), F32), jax.ShapeDtypeStruct((g, 1, p), F32)],
        compiler_params=pltpu.CompilerParams(dimension_semantics=("parallel",)),
        name="s5_operators",
    )(*rows, *cols, *bt, *ct)

    n = S5_STEP * c
    m_in = jnp.concatenate([a_re[:, ::-1], a_im[:, ::-1]], axis=-1).reshape(g, n, 2 * p)
    lag = jnp.arange(S5_STEP)[None, :] - jnp.arange(S5_STEP)[:, None]
    blocks = jnp.where((lag >= 0)[None, :, :, None, None], kt[:, jnp.maximum(lag, 0)], 0.0)
    m_intra = blocks.transpose(0, 1, 3, 2, 4).reshape(g, n, n)
    m_out = jnp.concatenate([mo_re.transpose(0, 2, 1, 3).reshape(g, p, n),
                             mo_im.transpose(0, 2, 1, 3).reshape(g, p, n)], axis=1)

    def pair(m, split_rows, split_cols):
        m = m.reshape(g // 2, 2, n, n)
        z = jnp.zeros_like(m[:, 0])
        full = jnp.concatenate([jnp.concatenate([m[:, 0], z], axis=2),
                                jnp.concatenate([z, m[:, 1]], axis=2)], axis=1)
        order = jnp.concatenate([jnp.arange(p), 2 * p + jnp.arange(p), p + jnp.arange(p), 3 * p + jnp.arange(p)])
        if split_rows:
            full = full[:, order, :]
        if split_cols:
            full = full[:, :, order]
        return full.astype(BF16)

    l8 = jnp.concatenate([l8_re.reshape(g // 2, 1, 2 * p), l8_im.reshape(g // 2, 1, 2 * p)], axis=-1)
    return pair(m_in, False, True), pair(m_intra, False, False), pair(m_out, True, False), l8


def _s5_body(batch, u_ref, min_ref, mintra_ref, mout_ref, l8_ref, y_ref, s_ref, h_ref):
    rows = u_ref.shape[0]
    half = S5_PAIR // 2
    pairs = S5_LANES // S5_PAIR
    for pp in range(pairs):
        lanes = slice(pp * S5_PAIR, (pp + 1) * S5_PAIR)
        s_ref[:, lanes] = _dot(u_ref[:, lanes], min_ref[pp])

    l8 = [(l8_ref[pp, :, :half], l8_ref[pp, :, half:]) for pp in range(pairs)]

    def step(n, carry):
        r0 = pl.multiple_of(n * batch, batch)
        new = []
        for pp in range(pairs):
            st_re, st_im = carry[pp]
            re_lanes = slice(pp * S5_PAIR, pp * S5_PAIR + half)
            im_lanes = slice(pp * S5_PAIR + half, (pp + 1) * S5_PAIR)
            h_ref[pl.ds(r0, batch), re_lanes] = st_re.astype(BF16)
            h_ref[pl.ds(r0, batch), im_lanes] = st_im.astype(BF16)
            lr, li = l8[pp]
            new.append((lr * st_re - li * st_im + s_ref[pl.ds(r0, batch), re_lanes],
                        lr * st_im + li * st_re + s_ref[pl.ds(r0, batch), im_lanes]))
        return tuple(new)

    zero = jnp.zeros((batch, half), F32)
    lax.fori_loop(0, rows // batch, step, tuple((zero, zero) for _ in range(pairs)), unroll=S5_SCAN_UNROLL)

    for pp in range(pairs):
        lanes = slice(pp * S5_PAIR, (pp + 1) * S5_PAIR)
        y_ref[:, lanes] = _dot(u_ref[:, lanes], mintra_ref[pp]) + _dot(h_ref[:, lanes], mout_ref[pp])


def _s5_scan(u, batch, m_in, m_intra, m_out, l8):
    rows, width = u.shape
    pairs = S5_LANES // S5_PAIR
    lane_spec = pl.BlockSpec((rows, S5_LANES), lambda i: (0, i))
    op_spec = pl.BlockSpec((pairs, S5_PAIR, S5_PAIR), lambda i: (i, 0, 0))
    return pl.pallas_call(
        functools.partial(_s5_body, batch),
        grid=(width // S5_LANES,),
        in_specs=[lane_spec, op_spec, op_spec, op_spec, pl.BlockSpec((pairs, 1, S5_PAIR), lambda i: (i, 0, 0))],
        out_specs=lane_spec,
        out_shape=jax.ShapeDtypeStruct((rows, width), F32),
        scratch_shapes=[pltpu.VMEM((rows, S5_LANES), F32), pltpu.VMEM((rows, S5_LANES), BF16)],
        compiler_params=pltpu.CompilerParams(dimension_semantics=("parallel",),
                                             vmem_limit_bytes=VMEM_LIMIT_BYTES),
        name="s5_scan",
    )(u, m_in, m_intra, m_out, l8)


def _glu_ffn_body(x_ref, ys_ref, gmix_ref, d_ref, wab_ref, g_ref, w1_ref, w3_ref, w2_ref, g2_ref, out_ref, scr_ref):
    batch, steps = x_ref.shape[0], x_ref.shape[1]
    pitch = steps + PITCH_PAD
    lane_blocks = D_MODEL // 128
    step = pl.program_id(0)

    def relayout_next(lb):
        for c in range(steps // S5_STEP):
            groups = [ys_ref[c, :, (lb * 8 + g8) * 128:(lb * 8 + g8 + 1) * 128] for g8 in range(8)]
            for s, blk in enumerate(_block_transpose(groups)):
                scr_ref[lb, pl.ds(c * S5_STEP + s, batch, stride=pitch), :] = blk

    @pl.when(step == 0)
    def _():
        for lb in range(lane_blocks):
            relayout_next(lb)

    @pl.when(step > 0)
    def _():
        x = x_ref[...].reshape(-1, D_MODEL)
        ys = jnp.concatenate(
            [jnp.concatenate([scr_ref[lb, b * pitch:b * pitch + steps, :] for b in range(batch)], axis=0)
             for lb in range(lane_blocks)], axis=1)
        y = ys + d_ref[...] * _rms(x, gmix_ref[...])
        ab = _dot(jax.nn.gelu(y).astype(BF16), wab_ref[...])
        x = x + ab[:, :D_MODEL] * jax.nn.sigmoid(ab[:, D_MODEL:])
        xn = _rms(x, g_ref[...]).astype(BF16)
        acc = jnp.zeros(x.shape, F32)
        for ci, f in enumerate(range(0, D_FF, FFN_COLS)):
            h1 = _dot(xn, w1_ref[:, f:f + FFN_COLS])
            h3 = _dot(xn, w3_ref[:, f:f + FFN_COLS])
            if ci < lane_blocks:
                relayout_next(ci)
            gate = (jax.nn.silu(h1) * h3).astype(BF16)
            acc = acc + _dot(gate, w2_ref[f:f + FFN_COLS, :])
        out_ref[...] = _rms(x + 0.5 * acc, g2_ref[...]).reshape(out_ref.shape)


def _glu_ffn(x, ys, g_mix, d, w_ab, g, w1, w3, w2, g2, layer, half):
    batch, seq, _ = x.shape
    steps = GLU_ROWS // batch
    tiles = seq // steps
    row_spec = pl.BlockSpec((batch, steps, D_MODEL), lambda j: (0, jnp.maximum(j - 1, 0), 0))
    ys_spec = pl.BlockSpec((steps // S5_STEP, batch, S5_STEP * D_MODEL), lambda j: (jnp.minimum(j, tiles - 1), 0, 0))

    def weight_spec(rows, cols):
        return pl.BlockSpec((None, None, rows, cols), lambda j: (layer, half, 0, 0), pipeline_mode=pl.Buffered(1))

    return pl.pallas_call(
        _glu_ffn_body,
        grid=(tiles + 1,),
        in_specs=[row_spec, ys_spec, _const_spec((1, D_MODEL)), _const_spec((1, D_MODEL)),
                  _const_spec((D_MODEL, 2 * D_MODEL)), _const_spec((1, D_MODEL)),
                  weight_spec(D_MODEL, D_FF), weight_spec(D_MODEL, D_FF), weight_spec(D_FF, D_MODEL),
                  _const_spec((1, D_MODEL))],
        out_specs=row_spec,
        out_shape=jax.ShapeDtypeStruct(x.shape, F32),
        scratch_shapes=[pltpu.VMEM((D_MODEL // 128, batch * (steps + PITCH_PAD), 128), F32)],
        compiler_params=pltpu.CompilerParams(dimension_semantics=("arbitrary",),
                                             vmem_limit_bytes=VMEM_LIMIT_BYTES),
        name="glu_ffn_final",
    )(x, ys, g_mix, d, w_ab, g, w1, w3, w2, g2)


def kernel(x, ffn_norm_g, ffn_w1, ffn_w3, ffn_w2, mix_norm_g, w_in_even, w_out_even, ret_norm_g, conv_w, conv_b, lru_w_a, lru_b_a, lru_w_i, lru_b_i, lru_lambda, s5_lambda_re, s5_lambda_im, s5_log_dt, s5_b_re, s5_b_im, s5_c_re, s5_c_im, s5_d, glu_w_a, glu_w_b, final_norm_g):
    batch, seq, _ = x.shape
    t = batch * seq
    chunk_steps = FFN_ROWS // batch
    assert FFN_ROWS == GLU_ROWS and FFN_ROWS % batch == 0 and batch % 8 == 0
    assert seq % MIX_ROWS == 0 and t % PLAIN_FFN_ROWS == 0 and chunk_steps % S5_STEP == 0 and seq % chunk_steps == 0
    assert S5_GROUPS % S5_OPS_GROUPS == 0
    row = lambda v: v.reshape(1, -1).astype(F32)
    w1, w3, w2 = ffn_w1.astype(BF16), ffn_w3.astype(BF16), ffn_w2.astype(BF16)

    def ffn(xv, layer, half, g2, post):
        return _ffn(xv, row(ffn_norm_g[layer, half]), w1, w3, w2, row(g2), post, layer, half)

    (xf,) = ffn(x.reshape(t, D_MODEL), 0, 0, final_norm_g, "plain")
    w_gates = jnp.concatenate([lru_w_a[0], lru_w_i[0]], axis=-1).astype(BF16)
    xf = _even_mixer(xf, batch, seq, row(mix_norm_g[0]), w_in_even[0].astype(BF16), w_out_even[0].astype(BF16),
                     row(ret_norm_g[0]), conv_w[0].astype(F32), row(conv_b[0]), w_gates,
                     row(lru_b_a[0]), row(lru_b_i[0]), row(lru_lambda[0]))
    (xf,) = ffn(xf, 0, 1, final_norm_g, "plain")

    x3, u = ffn(xf.reshape(batch, seq, D_MODEL), 1, 0, mix_norm_g[1], "chunked")
    chunks = seq // S5_STEP
    m_in, m_intra, m_out, l8 = _s5_operators(s5_lambda_re[0], s5_lambda_im[0], s5_log_dt[0],
                                             s5_b_re[0], s5_b_im[0], s5_c_re[0], s5_c_im[0])
    ys = _s5_scan(u.reshape(chunks * batch, S5_STEP * D_MODEL), batch, m_in, m_intra, m_out, l8)
    w_ab = jnp.concatenate([glu_w_a[0], glu_w_b[0]], axis=-1).astype(BF16)
    return _glu_ffn(x3, ys.reshape(chunks, batch, S5_STEP * D_MODEL), row(mix_norm_g[1]), row(s5_d[0]), w_ab,
                    row(ffn_norm_g[1, 1]), w1, w3, w2, row(final_norm_g), 1, 1)
```

```python
import functools
import math

import jax
import jax.numpy as jnp
from jax import lax
from jax.experimental import pallas as pl
from jax.experimental.pallas import tpu as pltpu

F32 = jnp.float32
BF16 = jnp.bfloat16

D_MODEL = 1024
D_FF = 2816
EPS = 1e-6
RET_HEADS = 4
HEAD_DIM = 128
RET_WIDTH = RET_HEADS * HEAD_DIM
ROPE_BASE = 10000.0
LRU_WIDTH = 512
LRU_BLOCKS = 4
LRU_BLOCK_DIM = 128
CONV_WIDTH = 4
LRU_C = 8.0
IN_EVEN_WIDTH = 4 * RET_WIDTH + 2 * LRU_WIDTH
S5_GROUP = 16
S5_GROUPS = 64
S5_STATE = 64
S5_STEP = 8
S5_PAIR = 2 * S5_STEP * S5_GROUP

VMEM_LIMIT_BYTES = 56 * 1024 * 1024

FFN_ROWS = 512
PLAIN_FFN_ROWS = 1024
FFN_COLS = 256
MIX_ROWS = 256
GLU_ROWS = 512
S5_LANES = 512
S5_OPS_GROUPS = 8
S5_SCAN_UNROLL = 8
PITCH_PAD = 8


def _rms(x, g):
    return x * lax.rsqrt(jnp.mean(x * x, axis=-1, keepdims=True) + EPS) * g


def _dot(a, b):
    return jnp.dot(a, b, preferred_element_type=F32)


def _const_spec(shape):
    nd = len(shape)
    return pl.BlockSpec(shape, lambda *_: (0,) * nd, pipeline_mode=pl.Buffered(1))


def _block_transpose(v):
    lane_block = lax.broadcasted_iota(jnp.int32, v[0].shape, 1) // S5_GROUP
    for k in (4, 2, 1):
        upper = (lane_block & k) != 0
        new = list(v)
        for a in range(8):
            if a & k == 0:
                new[a] = jnp.where(upper, pltpu.roll(v[a + k], S5_GROUP * k, 1), v[a])
                new[a + k] = jnp.where(upper, v[a + k], pltpu.roll(v[a], 128 - S5_GROUP * k, 1))
        v = new
    return v


def _ffn_body(post, x_ref, g_ref, w1_ref, w3_ref, w2_ref, g2_ref, *out_refs):
    lane_blocks = D_MODEL // 128

    def tile(between_chunks):
        x = x_ref[...].reshape(-1, D_MODEL)
        xn = _rms(x, g_ref[...]).astype(BF16)
        acc = jnp.zeros(x.shape, F32)
        for ci, f in enumerate(range(0, D_FF, FFN_COLS)):
            h1 = _dot(xn, w1_ref[:, f:f + FFN_COLS])
            h3 = _dot(xn, w3_ref[:, f:f + FFN_COLS])
            if between_chunks is not None and ci < lane_blocks:
                between_chunks(ci)
            gate = (jax.nn.silu(h1) * h3).astype(BF16)
            acc = acc + _dot(gate, w2_ref[f:f + FFN_COLS, :])
        y = x + 0.5 * acc
        out_refs[0][...] = y.reshape(out_refs[0].shape)
        return y

    if post == "plain":
        tile(None)
        return

    u_ref, scr_ref = out_refs[1], out_refs[2]
    batch, steps = x_ref.shape[0], x_ref.shape[1]
    pitch = steps + PITCH_PAD
    step = pl.program_id(0)
    last = pl.num_programs(0) - 1

    def relayout_previous(lb):
        for c in range(steps // S5_STEP):
            per_step = [scr_ref[lb, pl.ds(c * S5_STEP + s, batch, stride=pitch), :]
                        for s in range(S5_STEP)]
            for g8, blk in enumerate(_block_transpose(per_step)):
                lane0 = (lb * 8 + g8) * 128
                u_ref[c, :, lane0:lane0 + 128] = blk.astype(BF16)

    @pl.when(step == 0)
    def _():
        scr_ref[...] = jnp.zeros(scr_ref.shape, F32)

    @pl.when(step < last)
    def _():
        hn = _rms(tile(relayout_previous), g2_ref[...])
        for lb in range(lane_blocks):
            for b in range(batch):
                scr_ref[lb, b * pitch:b * pitch + steps, :] = hn[b * steps:(b + 1) * steps, lb * 128:(lb + 1) * 128]

    @pl.when(step == last)
    def _():
        for lb in range(lane_blocks):
            relayout_previous(lb)


def _ffn(x, g, w1, w3, w2, g2, post, layer, half):
    scratch = []
    if post == "chunked":
        batch, seq, _ = x.shape
        steps = FFN_ROWS // batch
        tiles = seq // steps
        grid = tiles + 1
        semantics = "arbitrary"
        row_spec = pl.BlockSpec((batch, steps, D_MODEL), lambda i: (0, jnp.minimum(i, tiles - 1), 0))
        out_shape = [jax.ShapeDtypeStruct(x.shape, F32),
                     jax.ShapeDtypeStruct((seq // S5_STEP, batch, S5_STEP * D_MODEL), BF16)]
        out_specs = [row_spec, pl.BlockSpec((steps // S5_STEP, batch, S5_STEP * D_MODEL),
                                            lambda i: (jnp.maximum(i - 1, 0), 0, 0))]
        scratch.append(pltpu.VMEM((D_MODEL // 128, batch * (steps + PITCH_PAD), 128), F32))
    else:
        grid = x.shape[0] // PLAIN_FFN_ROWS
        semantics = "parallel"
        row_spec = pl.BlockSpec((PLAIN_FFN_ROWS, D_MODEL), lambda i: (i, 0))
        out_shape = [jax.ShapeDtypeStruct(x.shape, F32)]
        out_specs = [row_spec]

    def weight_spec(rows, cols):
        return pl.BlockSpec((None, None, rows, cols), lambda i: (layer, half, 0, 0), pipeline_mode=pl.Buffered(1))

    return pl.pallas_call(
        functools.partial(_ffn_body, post),
        grid=(grid,),
        in_specs=[row_spec, _const_spec((1, D_MODEL)), weight_spec(D_MODEL, D_FF), weight_spec(D_MODEL, D_FF),
                  weight_spec(D_FF, D_MODEL), _const_spec((1, D_MODEL))],
        out_specs=out_specs,
        out_shape=out_shape,
        scratch_shapes=scratch,
        compiler_params=pltpu.CompilerParams(dimension_semantics=(semantics,),
                                             vmem_limit_bytes=VMEM_LIMIT_BYTES),
        name="ffn_" + post,
    )(x, g, w1, w3, w2, g2)


def _even_body(x_ref, gmix_ref, win_ref, wout_ref, rg_ref, cw_ref, cb_ref, wg_ref, ba_ref, bi_ref,
               lam_ref, cos_ref, sin_ref, dec_ref, qdec_ref, kdec_ref, cdec_ref,
               out_ref, r_ref, hc_ref, xl_ref, merged_ref):
    tc = x_ref.shape[0]

    @pl.when(pl.program_id(1) == 0)
    def _():
        r_ref[...] = jnp.zeros(r_ref.shape, F32)
        hc_ref[...] = jnp.zeros(hc_ref.shape, F32)
        xl_ref[0:8, :] = jnp.zeros((8, LRU_WIDTH), F32)

    x = x_ref[...]
    h = _rms(x, gmix_ref[...]).astype(BF16)
    proj = _dot(h, win_ref[...])

    cos = cos_ref[...]
    sin = sin_ref[...]
    for hd in range(RET_HEADS):
        lanes = slice(hd * HEAD_DIM, (hd + 1) * HEAD_DIM)
        q = proj[:, hd * HEAD_DIM:(hd + 1) * HEAD_DIM]
        k = proj[:, RET_WIDTH + hd * HEAD_DIM:RET_WIDTH + (hd + 1) * HEAD_DIM]
        v = proj[:, 2 * RET_WIDTH + hd * HEAD_DIM:2 * RET_WIDTH + (hd + 1) * HEAD_DIM]
        g_ret = proj[:, 3 * RET_WIDTH + hd * HEAD_DIM:3 * RET_WIDTH + (hd + 1) * HEAD_DIM]
        q = q * cos + pltpu.roll(q, HEAD_DIM // 2, 1) * sin
        k = (k * cos + pltpu.roll(k, HEAD_DIM // 2, 1) * sin) * (HEAD_DIM ** -0.5)
        vb = v.astype(BF16)
        scores = lax.dot_general(q.astype(BF16), k.astype(BF16), (((1,), (1,)), ((), ())),
                                 preferred_element_type=F32) * dec_ref[hd]
        state = r_ref[hd]
        ret = _dot(scores.astype(BF16), vb) + _dot((q * qdec_ref[:, lanes]).astype(BF16), state.astype(BF16))
        kv = lax.dot_general((k * kdec_ref[:, lanes]).astype(BF16), vb, (((0,), (0,)), ((), ())),
                             preferred_element_type=F32)
        r_ref[hd] = state * cdec_ref[:, lanes] + kv
        mu = jnp.mean(ret, axis=-1, keepdims=True)
        cen = ret - mu
        var = jnp.mean(cen * cen, axis=-1, keepdims=True)
        normed = cen * lax.rsqrt(var + EPS) * rg_ref[:, lanes]
        merged_ref[:, lanes] = (normed * jax.nn.silu(g_ret)).astype(BF16)

    x_lru = proj[:, 4 * RET_WIDTH:4 * RET_WIDTH + LRU_WIDTH]
    g_lru = proj[:, 4 * RET_WIDTH + LRU_WIDTH:]
    xl_ref[8:8 + tc, :] = x_lru
    xc = cb_ref[...]
    for tap in range(CONV_WIDTH):
        start = 8 - (CONV_WIDTH - 1) + tap
        xc = xc + xl_ref[start:start + tc, :] * cw_ref[tap:tap + 1, :]
    xl_ref[0:8, :] = xl_ref[tc:tc + 8, :]
    lam = lam_ref[...]
    softplus_neg = jnp.maximum(-lam, 0.0) + jnp.log1p(jnp.exp(-jnp.abs(lam)))
    rows = lax.broadcasted_iota(jnp.int32, (tc // 8, 8, LRU_BLOCK_DIM), 1)
    for blk in range(LRU_BLOCKS):
        lanes = slice(blk * LRU_BLOCK_DIM, (blk + 1) * LRU_BLOCK_DIM)
        xb = xc[:, lanes]
        gates = _dot(xb.astype(BF16), wg_ref[blk])
        r = jax.nn.sigmoid(gates[:, :LRU_BLOCK_DIM] + ba_ref[:, lanes])
        i = jax.nn.sigmoid(gates[:, LRU_BLOCK_DIM:] + bi_ref[:, lanes])
        log_a = -LRU_C * r * softplus_neg[:, lanes]
        a = jnp.exp(log_a)
        mult = jnp.sqrt((1.0 + a * a) * jnp.tanh(-log_a))
        a = a.reshape(tc // 8, 8, LRU_BLOCK_DIM)
        b = (mult * i * xb).reshape(tc // 8, 8, LRU_BLOCK_DIM)
        for d in (1, 2, 4):
            keep = rows >= d
            a_prev = jnp.where(keep, pltpu.roll(a, d, 1), 1.0)
            b_prev = jnp.where(keep, pltpu.roll(b, d, 1), 0.0)
            b = a * b_prev + b
            a = a * a_prev
        carry = hc_ref[:, lanes]
        groups = []
        for j in range(tc // 8):
            hj = a[j] * carry + b[j]
            carry = hj[7:8, :]
            groups.append(hj)
        hc_ref[:, lanes] = carry
        hseq = jnp.concatenate(groups, axis=0)
        lru_lanes = slice(RET_WIDTH + blk * LRU_BLOCK_DIM, RET_WIDTH + (blk + 1) * LRU_BLOCK_DIM)
        merged_ref[:, lru_lanes] = (hseq * jax.nn.gelu(g_lru[:, lanes])).astype(BF16)

    out_ref[...] = x + _dot(merged_ref[...], wout_ref[...])


def _rope_tables(seq):
    half = HEAD_DIM // 2
    inv = ROPE_BASE ** (-jnp.arange(half, dtype=F32) / half)
    ang = jnp.arange(seq, dtype=F32)[:, None] * inv[None, :]
    cos = jnp.cos(ang)
    sin = jnp.sin(ang)
    return jnp.concatenate([cos, cos], axis=-1), jnp.concatenate([-sin, sin], axis=-1)


def _decay_tables(chunk):
    log_gamma = jnp.log1p(-jnp.power(2.0, -5.0 - jnp.arange(RET_HEADS, dtype=F32)))
    pos = jnp.arange(chunk, dtype=F32)
    diff = pos[:, None] - pos[None, :]
    decay = jnp.where(diff >= 0, jnp.exp(log_gamma[:, None, None] * jnp.maximum(diff, 0.0)), 0.0)
    k_decay = jnp.exp(log_gamma[:, None] * (chunk - 1.0 - pos)[None, :])
    q_decay = jnp.exp(log_gamma[:, None] * (pos + 1.0)[None, :])
    chunk_decay = jnp.exp(log_gamma * chunk)

    def lanes(t):
        return jnp.repeat(t.T, HEAD_DIM, axis=1)

    return decay, lanes(q_decay), lanes(k_decay), lanes(chunk_decay[:, None])


def _even_mixer(x, batch, seq, g_mix, w_in, w_out, ret_g, conv_w, conv_b, w_gates, b_a, b_i, lam):
    tc = MIX_ROWS
    tiles = seq // tc
    cos, sin = _rope_tables(seq)
    decay, q_decay, k_decay, chunk_decay = _decay_tables(tc)
    row_spec = pl.BlockSpec((tc, D_MODEL), lambda b, j: (b * tiles + j, 0))
    pos_spec = pl.BlockSpec((tc, HEAD_DIM), lambda b, j: (j, 0))
    return pl.pallas_call(
        _even_body,
        grid=(batch, tiles),
        in_specs=[row_spec, _const_spec((1, D_MODEL)), _const_spec((D_MODEL, IN_EVEN_WIDTH)),
                  _const_spec((D_MODEL, D_MODEL)), _const_spec((1, RET_WIDTH)),
                  _const_spec((CONV_WIDTH, LRU_WIDTH)), _const_spec((1, LRU_WIDTH)),
                  _const_spec((LRU_BLOCKS, LRU_BLOCK_DIM, 2 * LRU_BLOCK_DIM)),
                  _const_spec((1, LRU_WIDTH)), _const_spec((1, LRU_WIDTH)), _const_spec((1, LRU_WIDTH)),
                  pos_spec, pos_spec, _const_spec((RET_HEADS, tc, tc)),
                  _const_spec((tc, RET_WIDTH)), _const_spec((tc, RET_WIDTH)), _const_spec((1, RET_WIDTH))],
        out_specs=row_spec,
        out_shape=jax.ShapeDtypeStruct(x.shape, F32),
        scratch_shapes=[pltpu.VMEM((RET_HEADS, HEAD_DIM, HEAD_DIM), F32),
                        pltpu.VMEM((1, LRU_WIDTH), F32),
                        pltpu.VMEM((tc + 8, LRU_WIDTH), F32),
                        pltpu.VMEM((tc, D_MODEL), BF16)],
        compiler_params=pltpu.CompilerParams(dimension_semantics=("parallel", "arbitrary"),
                                             vmem_limit_bytes=VMEM_LIMIT_BYTES),
        name="even_mixer",
    )(x, g_mix, w_in, w_out, ret_g, conv_w, conv_b, w_gates, b_a, b_i, lam,
      cos, sin, decay, q_decay, k_decay, chunk_decay)


def _s5_discretize(lam_re, lam_im, log_dt):
    dt = jnp.exp(log_dt)
    mag = jnp.exp(lam_re * dt)
    lbar_re = mag * jnp.cos(lam_im * dt)
    lbar_im = mag * jnp.sin(lam_im * dt)
    den = lam_re * lam_re + lam_im * lam_im
    nr = lbar_re - 1.0
    ni = lbar_im
    f_re = (nr * lam_re + ni * lam_im) / den
    f_im = (ni * lam_re - nr * lam_im) / den
    return lbar_re, lbar_im, f_re, f_im


def _s5_ops_body(lre_row_ref, lim_row_ref, ldt_row_ref, lre_col_ref, lim_col_ref, ldt_col_ref,
                 bt_re_ref, bt_im_ref, ct_re_ref, ct_im_ref,
                 a_re_ref, a_im_ref, k_ref, mo_re_ref, mo_im_ref, l8_re_ref, l8_im_ref):
    lr, li, f_re, f_im = _s5_discretize(lre_row_ref[...], lim_row_ref[...], ldt_row_ref[...])
    bt_re = bt_re_ref[...]
    bt_im = bt_im_ref[...]
    bbar_re = f_re * bt_re - f_im * bt_im
    bbar_im = f_re * bt_im + f_im * bt_re
    ct_re = ct_re_ref[...]
    ct_im = ct_im_ref[...]
    pr = jnp.ones_like(lr)
    pi = jnp.zeros_like(li)
    for tau in range(S5_STEP):
        a_re = bbar_re * pr - bbar_im * pi
        a_im = bbar_re * pi + bbar_im * pr
        a_re_ref[:, tau] = a_re
        a_im_ref[:, tau] = a_im
        for gi in range(S5_OPS_GROUPS):
            k_ref[gi, tau] = (
                jnp.dot(a_re[gi], ct_re[gi], precision=lax.Precision.HIGHEST, preferred_element_type=F32)
                - jnp.dot(a_im[gi], ct_im[gi], precision=lax.Precision.HIGHEST, preferred_element_type=F32))
        pr, pi = pr * lr - pi * li, pr * li + pi * lr
    l8_re_ref[...] = pr
    l8_im_ref[...] = pi

    lr_c, li_c, _, _ = _s5_discretize(lre_col_ref[...], lim_col_ref[...], ldt_col_ref[...])
    qr = lr_c
    qi = li_c
    for t in range(S5_STEP):
        mo_re_ref[:, t] = ct_re * qr - ct_im * qi
        mo_im_ref[:, t] = -(ct_re * qi) - ct_im * qr
        qr, qi = qr * lr_c - qi * li_c, qr * li_c + qi * lr_c


def _s5_operators(lam_re, lam_im, log_dt, b_re, b_im, c_re, c_im):
    g, p, c = S5_GROUPS, S5_STATE, S5_GROUP
    ldt = jnp.broadcast_to(log_dt[:, None], (g, p))
    rows = [a.reshape(g, 1, p) for a in (lam_re, lam_im, ldt)]
    cols = [a.reshape(g, p, 1) for a in (lam_re, lam_im, ldt)]
    bt = [jnp.swapaxes(a, 1, 2) for a in (b_re, b_im)]
    ct = [jnp.swapaxes(a, 1, 2) for a in (c_re, c_im)]

    def spec(*shape):
        nd = len(shape)
        return pl.BlockSpec((S5_OPS_GROUPS,) + shape, lambda i: (i,) + (0,) * nd)

    a_re, a_im, kt, mo_re, mo_im, l8_re, l8_im = pl.pallas_call(
        _s5_ops_body,
        grid=(g // S5_OPS_GROUPS,),
        in_specs=[spec(1, p)] * 3 + [spec(p, 1)] * 3 + [spec(c, p)] * 2 + [spec(p, c)] * 2,
        out_specs=[spec(S5_STEP, c, p), spec(S5_STEP, c, p), spec(S5_STEP, c, c),
                   spec(S5_STEP, p, c), spec(S5_STEP, p, c), spec(1, p), spec(1, p)],
        out_shape=[jax.ShapeDtypeStruct((g, S5_STEP, c, p), F32), jax.ShapeDtypeStruct((g, S5_STEP, c, p), F32),
                   jax.ShapeDtypeStruct((g, S5_STEP, c, c), F32),
                   jax.ShapeDtypeStruct((g, S5_STEP, p, c), F32), jax.ShapeDtypeStruct((g, S5_STEP, p, c), F32),
                   jax.ShapeDtypeStruct((g, 1, p), F32), jax.ShapeDtypeStruct((g, 1, p), F32)],
        compiler_params=pltpu.CompilerParams(dimension_semantics=("parallel",)),
        name="s5_operators",
    )(*rows, *cols, *bt, *ct)

    n = S5_STEP * c
    h = g // 2
    a_re, a_im, kt, mo_re, mo_im = (v.astype(BF16) for v in (a_re, a_im, kt, mo_re, mo_im))

    def block_diag(first, second):
        z = jnp.zeros_like(first)
        return jnp.concatenate([jnp.concatenate([first, z], axis=2), jnp.concatenate([z, second], axis=2)], axis=1)

    in_re = a_re[:, ::-1].reshape(h, 2, n, p)
    in_im = a_im[:, ::-1].reshape(h, 2, n, p)
    m_in = jnp.concatenate([block_diag(in_re[:, 0], in_re[:, 1]), block_diag(in_im[:, 0], in_im[:, 1])], axis=2)
    zero_block = jnp.zeros((g, c, c), BF16)
    rows = [jnp.concatenate([zero_block] * s + [kt[:, tau] for tau in range(S5_STEP - s)], axis=-1)
            for s in range(S5_STEP)]
    intra = jnp.stack(rows, axis=1).reshape(h, 2, n, n)
    m_intra = block_diag(intra[:, 0], intra[:, 1])
    out_re = mo_re.transpose(0, 2, 1, 3).reshape(h, 2, p, n)
    out_im = mo_im.transpose(0, 2, 1, 3).reshape(h, 2, p, n)
    m_out = jnp.concatenate([block_diag(out_re[:, 0], out_re[:, 1]), block_diag(out_im[:, 0], out_im[:, 1])], axis=1)

    l8 = jnp.concatenate([l8_re.reshape(h, 1, 2 * p), l8_im.reshape(h, 1, 2 * p)], axis=-1)
    return m_in, m_intra, m_out, l8


def _s5_body(batch, u_ref, min_ref, mintra_ref, mout_ref, l8_ref, y_ref, s_ref, h_ref):
    rows = u_ref.shape[0]
    half = S5_PAIR // 2
    pairs = S5_LANES // S5_PAIR
    for pp in range(pairs):
        lanes = slice(pp * S5_PAIR, (pp + 1) * S5_PAIR)
        s_ref[:, lanes] = _dot(u_ref[:, lanes], min_ref[pp])

    l8 = [(l8_ref[pp, :, :half], l8_ref[pp, :, half:]) for pp in range(pairs)]

    def step(n, carry):
        r0 = pl.multiple_of(n * batch, batch)
        new = []
        for pp in range(pairs):
            st_re, st_im = carry[pp]
            re_lanes = slice(pp * S5_PAIR, pp * S5_PAIR + half)
            im_lanes = slice(pp * S5_PAIR + half, (pp + 1) * S5_PAIR)
            h_ref[pl.ds(r0, batch), re_lanes] = st_re.astype(BF16)
            h_ref[pl.ds(r0, batch), im_lanes] = st_im.astype(BF16)
            lr, li = l8[pp]
            new.append((lr * st_re - li * st_im + s_ref[pl.ds(r0, batch), re_lanes],
                        lr * st_im + li * st_re + s_ref[pl.ds(r0, batch), im_lanes]))
        return tuple(new)

    zero = jnp.zeros((batch, half), F32)
    lax.fori_loop(0, rows // batch, step, tuple((zero, zero) for _ in range(pairs)), unroll=S5_SCAN_UNROLL)

    for pp in range(pairs):
        lanes = slice(pp * S5_PAIR, (pp + 1) * S5_PAIR)
        y_ref[:, lanes] = _dot(u_ref[:, lanes], mintra_ref[pp]) + _dot(h_ref[:, lanes], mout_ref[pp])


def _s5_scan(u, batch, m_in, m_intra, m_out, l8):
    rows, width = u.shape
    pairs = S5_LANES // S5_PAIR
    lane_spec = pl.BlockSpec((rows, S5_LANES), lambda i: (0, i))
    op_spec = pl.BlockSpec((pairs, S5_PAIR, S5_PAIR), lambda i: (i, 0, 0))
    return pl.pallas_call(
        functools.partial(_s5_body, batch),
        grid=(width // S5_LANES,),
        in_specs=[lane_spec, op_spec, op_spec, op_spec, pl.BlockSpec((pairs, 1, S5_PAIR), lambda i: (i, 0, 0))],
        out_specs=lane_spec,
        out_shape=jax.ShapeDtypeStruct((rows, width), F32),
        scratch_shapes=[pltpu.VMEM((rows, S5_LANES), F32), pltpu.VMEM((rows, S5_LANES), BF16)],
        compiler_params=pltpu.CompilerParams(dimension_semantics=("parallel",),
                                             vmem_limit_bytes=VMEM_LIMIT_BYTES),
        name="s5_scan",
    )(u, m_in, m_intra, m_out, l8)


def _glu_ffn_body(x_ref, ys_ref, gmix_ref, d_ref, wab_ref, g_ref, w1_ref, w3_ref, w2_ref, g2_ref, out_ref, scr_ref):
    batch, steps = x_ref.shape[0], x_ref.shape[1]
    pitch = steps + PITCH_PAD
    lane_blocks = D_MODEL // 128
    step = pl.program_id(0)

    def relayout_next(lb):
        for c in range(steps // S5_STEP):
            groups = [ys_ref[c, :, (lb * 8 + g8) * 128:(lb * 8 + g8 + 1) * 128] for g8 in range(8)]
            for s, blk in enumerate(_block_transpose(groups)):
                scr_ref[lb, pl.ds(c * S5_STEP + s, batch, stride=pitch), :] = blk

    @pl.when(step == 0)
    def _():
        for lb in range(lane_blocks):
            relayout_next(lb)

    @pl.when(step > 0)
    def _():
        x = x_ref[...].reshape(-1, D_MODEL)
        ys = jnp.concatenate(
            [jnp.concatenate([scr_ref[lb, b * pitch:b * pitch + steps, :] for b in range(batch)], axis=0)
             for lb in range(lane_blocks)], axis=1)
        y = ys + d_ref[...] * _rms(x, gmix_ref[...])
        ab = _dot(jax.nn.gelu(y).astype(BF16), wab_ref[...])
        x = x + ab[:, :D_MODEL] * jax.nn.sigmoid(ab[:, D_MODEL:])
        xn = _rms(x, g_ref[...]).astype(BF16)
        acc = jnp.zeros(x.shape, F32)
        for ci, f in enumerate(range(0, D_FF, FFN_COLS)):
            h1 = _dot(xn, w1_ref[:, f:f + FFN_COLS])
            h3 = _dot(xn, w3_ref[:, f:f + FFN_COLS])
            if ci < lane_blocks:
                relayout_next(ci)
            gate = (jax.nn.silu(h1) * h3).astype(BF16)
            acc = acc + _dot(gate, w2_ref[f:f + FFN_COLS, :])
        out_ref[...] = _rms(x + 0.5 * acc, g2_ref[...]).reshape(out_ref.shape)


def _glu_ffn(x, ys, g_mix, d, w_ab, g, w1, w3, w2, g2, layer, half):
    batch, seq, _ = x.shape
    steps = GLU_ROWS // batch
    tiles = seq // steps
    row_spec = pl.BlockSpec((batch, steps, D_MODEL), lambda j: (0, jnp.maximum(j - 1, 0), 0))
    ys_spec = pl.BlockSpec((steps // S5_STEP, batch, S5_STEP * D_MODEL), lambda j: (jnp.minimum(j, tiles - 1), 0, 0))

    def weight_spec(rows, cols):
        return pl.BlockSpec((None, None, rows, cols), lambda j: (layer, half, 0, 0), pipeline_mode=pl.Buffered(1))

    return pl.pallas_call(
        _glu_ffn_body,
        grid=(tiles + 1,),
        in_specs=[row_spec, ys_spec, _const_spec((1, D_MODEL)), _const_spec((1, D_MODEL)),
                  _const_spec((D_MODEL, 2 * D_MODEL)), _const_spec((1, D_MODEL)),
                  weight_spec(D_MODEL, D_FF), weight_spec(D_MODEL, D_FF), weight_spec(D_FF, D_MODEL),
                  _const_spec((1, D_MODEL))],
        out_specs=row_spec,
        out_shape=jax.ShapeDtypeStruct(x.shape, F32),
        scratch_shapes=[pltpu.VMEM((D_MODEL // 128, batch * (steps + PITCH_PAD), 128), F32)],
        compiler_params=pltpu.CompilerParams(dimension_semantics=("arbitrary",),
                                             vmem_limit_bytes=VMEM_LIMIT_BYTES),
        name="glu_ffn_final",
    )(x, ys, g_mix, d, w_ab, g, w1, w3, w2, g2)


def kernel(x, ffn_norm_g, ffn_w1, ffn_w3, ffn_w2, mix_norm_g, w_in_even, w_out_even, ret_norm_g, conv_w, conv_b, lru_w_a, lru_b_a, lru_w_i, lru_b_i, lru_lambda, s5_lambda_re, s5_lambda_im, s5_log_dt, s5_b_re, s5_b_im, s5_c_re, s5_c_im, s5_d, glu_w_a, glu_w_b, final_norm_g):
    batch, seq, _ = x.shape
    t = batch * seq
    chunk_steps = FFN_ROWS // batch
    assert FFN_ROWS == GLU_ROWS and FFN_ROWS % batch == 0 and batch % 8 == 0
    assert seq % MIX_ROWS == 0 and t % PLAIN_FFN_ROWS == 0 and chunk_steps % S5_STEP == 0 and seq % chunk_steps == 0
    assert S5_GROUPS % S5_OPS_GROUPS == 0
    row = lambda v: v.reshape(1, -1).astype(F32)
    w1, w3, w2 = ffn_w1.astype(BF16), ffn_w3.astype(BF16), ffn_w2.astype(BF16)

    def ffn(xv, layer, half, g2, post):
        return _ffn(xv, row(ffn_norm_g[layer, half]), w1, w3, w2, row(g2), post, layer, half)

    (xf,) = ffn(x.reshape(t, D_MODEL), 0, 0, final_norm_g, "plain")
    w_gates = jnp.concatenate([lru_w_a[0], lru_w_i[0]], axis=-1).astype(BF16)
    xf = _even_mixer(xf, batch, seq, row(mix_norm_g[0]), w_in_even[0].astype(BF16), w_out_even[0].astype(BF16),
                     row(ret_norm_g[0]), conv_w[0].astype(F32), row(conv_b[0]), w_gates,
                     row(lru_b_a[0]), row(lru_b_i[0]), row(lru_lambda[0]))
    (xf,) = ffn(xf, 0, 1, final_norm_g, "plain")

    x3, u = ffn(xf.reshape(batch, seq, D_MODEL), 1, 0, mix_norm_g[1], "chunked")
    chunks = seq // S5_STEP
    m_in, m_intra, m_out, l8 = _s5_operators(s5_lambda_re[0], s5_lambda_im[0], s5_log_dt[0],
                                             s5_b_re[0], s5_b_im[0], s5_c_re[0], s5_c_im[0])
    ys = _s5_scan(u.reshape(chunks * batch, S5_STEP * D_MODEL), batch, m_in, m_intra, m_out, l8)
    w_ab = jnp.concatenate([glu_w_a[0], glu_w_b[0]], axis=-1).astype(BF16)
    return _glu_ffn(x3, ys.reshape(chunks, batch, S5_STEP * D_MODEL), row(mix_norm_g[1]), row(s5_d[0]), w_ab,
                    row(ffn_norm_g[1, 1]), w1, w3, w2, row(final_norm_g), 1, 1)
```

```python
import functools

import jax
import jax.numpy as jnp
from jax import lax
from jax.experimental import pallas as pl
from jax.experimental.pallas import tpu as pltpu

F32 = jnp.float32
BF16 = jnp.bfloat16

D_MODEL = 1024
D_FF = 2816
EPS = 1e-6
RET_HEADS = 4
HEAD_DIM = 128
RET_WIDTH = RET_HEADS * HEAD_DIM
ROPE_BASE = 10000.0
LRU_WIDTH = 512
LRU_BLOCKS = 4
LRU_BLOCK_DIM = 128
CONV_WIDTH = 4
LRU_C = 8.0
IN_EVEN_WIDTH = 4 * RET_WIDTH + 2 * LRU_WIDTH
S5_GROUP = 16
S5_GROUPS = 64
S5_STATE = 64
S5_STEP = 8
S5_PAIR = 2 * S5_STEP * S5_GROUP

VMEM_LIMIT_BYTES = 56 * 1024 * 1024

FFN_ROWS = 512
PLAIN_FFN_ROWS = 1024
FFN_COLS = 256
MIX_ROWS = 256
GLU_ROWS = 512
S5_LANES = 512
S5_OPS_GROUPS = 8
S5_SCAN_UNROLL = 8
PITCH_PAD = 8


def _rms(x, g):
    return x * lax.rsqrt(jnp.mean(x * x, axis=-1, keepdims=True) + EPS) * g


def _dot(a, b):
    return jnp.dot(a, b, preferred_element_type=F32)


def _const_spec(shape):
    nd = len(shape)
    return pl.BlockSpec(shape, lambda *_: (0,) * nd, pipeline_mode=pl.Buffered(1))


def _block_transpose(v):
    lane_block = lax.broadcasted_iota(jnp.int32, v[0].shape, 1) // S5_GROUP
    for k in (4, 2, 1):
        upper = (lane_block & k) != 0
        new = list(v)
        for a in range(8):
            if a & k == 0:
                new[a] = jnp.where(upper, pltpu.roll(v[a + k], S5_GROUP * k, 1), v[a])
                new[a + k] = jnp.where(upper, v[a + k], pltpu.roll(v[a], 128 - S5_GROUP * k, 1))
        v = new
    return v


def _ffn_body(post, x_ref, g_ref, w1_ref, w3_ref, w2_ref, g2_ref, *out_refs):
    lane_blocks = D_MODEL // 128

    def tile(between_chunks):
        x = x_ref[...].reshape(-1, D_MODEL)
        xn = _rms(x, g_ref[...]).astype(BF16)
        acc = jnp.zeros(x.shape, F32)
        for ci, f in enumerate(range(0, D_FF, FFN_COLS)):
            h1 = _dot(xn, w1_ref[:, f:f + FFN_COLS])
            h3 = _dot(xn, w3_ref[:, f:f + FFN_COLS])
            if between_chunks is not None and ci < lane_blocks:
                between_chunks(ci)
            gate = (jax.nn.silu(h1) * h3).astype(BF16)
            acc = acc + _dot(gate, w2_ref[f:f + FFN_COLS, :])
        y = x + 0.5 * acc
        out_refs[0][...] = y.reshape(out_refs[0].shape)
        return y

    if post == "plain":
        tile(None)
        return

    u_ref, scr_ref = out_refs[1], out_refs[2]
    batch, steps = x_ref.shape[0], x_ref.shape[1]
    pitch = steps + PITCH_PAD
    step = pl.program_id(0)
    last = pl.num_programs(0) - 1

    def relayout_previous(lb):
        for c in range(steps // S5_STEP):
            per_step = [scr_ref[lb, pl.ds(c * S5_STEP + s, batch, stride=pitch), :]
                        for s in range(S5_STEP)]
            for g8, blk in enumerate(_block_transpose(per_step)):
                lane0 = (lb * 8 + g8) * 128
                u_ref[c, :, lane0:lane0 + 128] = blk.astype(BF16)

    @pl.when(step == 0)
    def _():
        scr_ref[...] = jnp.zeros(scr_ref.shape, F32)

    @pl.when(step < last)
    def _():
        hn = _rms(tile(relayout_previous), g2_ref[...])
        for lb in range(lane_blocks):
            for b in range(batch):
                scr_ref[lb, b * pitch:b * pitch + steps, :] = hn[b * steps:(b + 1) * steps, lb * 128:(lb + 1) * 128]

    @pl.when(step == last)
    def _():
        for lb in range(lane_blocks):
            relayout_previous(lb)


def _ffn(x, g, w1, w3, w2, g2, post, layer, half):
    scratch = []
    if post == "chunked":
        batch, seq, _ = x.shape
        steps = FFN_ROWS // batch
        tiles = seq // steps
        grid = tiles + 1
        semantics = "arbitrary"
        row_spec = pl.BlockSpec((batch, steps, D_MODEL), lambda i: (0, jnp.minimum(i, tiles - 1), 0))
        out_shape = [jax.ShapeDtypeStruct(x.shape, F32),
                     jax.ShapeDtypeStruct((seq // S5_STEP, batch, S5_STEP * D_MODEL), BF16)]
        out_specs = [row_spec, pl.BlockSpec((steps // S5_STEP, batch, S5_STEP * D_MODEL),
                                            lambda i: (jnp.maximum(i - 1, 0), 0, 0))]
        scratch.append(pltpu.VMEM((D_MODEL // 128, batch * (steps + PITCH_PAD), 128), F32))
    else:
        grid = x.shape[0] // PLAIN_FFN_ROWS
        semantics = "parallel"
        row_spec = pl.BlockSpec((PLAIN_FFN_ROWS, D_MODEL), lambda i: (i, 0))
        out_shape = [jax.ShapeDtypeStruct(x.shape, F32)]
        out_specs = [row_spec]

    def weight_spec(rows, cols):
        return pl.BlockSpec((None, None, rows, cols), lambda i: (layer, half, 0, 0), pipeline_mode=pl.Buffered(1))

    return pl.pallas_call(
        functools.partial(_ffn_body, post),
        grid=(grid,),
        in_specs=[row_spec, _const_spec((1, D_MODEL)), weight_spec(D_MODEL, D_FF), weight_spec(D_MODEL, D_FF),
                  weight_spec(D_FF, D_MODEL), _const_spec((1, D_MODEL))],
        out_specs=out_specs,
        out_shape=out_shape,
        scratch_shapes=scratch,
        compiler_params=pltpu.CompilerParams(dimension_semantics=(semantics,),
                                             vmem_limit_bytes=VMEM_LIMIT_BYTES),
        name="ffn_" + post,
    )(x, g, w1, w3, w2, g2)


def _even_body(x_ref, gmix_ref, win_ref, wout_ref, rg_ref, cw_ref, cb_ref, wg_ref, ba_ref, bi_ref,
               lam_ref, cos_ref, sin_ref, dec_ref, qdec_ref, kdec_ref, cdec_ref,
               out_ref, r_ref, hc_ref, xl_ref, merged_ref):
    tc = x_ref.shape[0]

    @pl.when(pl.program_id(1) == 0)
    def _():
        r_ref[...] = jnp.zeros(r_ref.shape, F32)
        hc_ref[...] = jnp.zeros(hc_ref.shape, F32)
        xl_ref[0:8, :] = jnp.zeros((8, LRU_WIDTH), F32)

    x = x_ref[...]
    h = _rms(x, gmix_ref[...]).astype(BF16)
    proj = _dot(h, win_ref[...])

    cos = cos_ref[...]
    sin = sin_ref[...]
    for hd in range(RET_HEADS):
        lanes = slice(hd * HEAD_DIM, (hd + 1) * HEAD_DIM)
        q = proj[:, hd * HEAD_DIM:(hd + 1) * HEAD_DIM]
        k = proj[:, RET_WIDTH + hd * HEAD_DIM:RET_WIDTH + (hd + 1) * HEAD_DIM]
        v = proj[:, 2 * RET_WIDTH + hd * HEAD_DIM:2 * RET_WIDTH + (hd + 1) * HEAD_DIM]
        g_ret = proj[:, 3 * RET_WIDTH + hd * HEAD_DIM:3 * RET_WIDTH + (hd + 1) * HEAD_DIM]
        q = q * cos + pltpu.roll(q, HEAD_DIM // 2, 1) * sin
        k = (k * cos + pltpu.roll(k, HEAD_DIM // 2, 1) * sin) * (HEAD_DIM ** -0.5)
        vb = v.astype(BF16)
        scores = lax.dot_general(q.astype(BF16), k.astype(BF16), (((1,), (1,)), ((), ())),
                                 preferred_element_type=F32) * dec_ref[hd]
        state = r_ref[hd]
        ret = _dot(scores.astype(BF16), vb) + _dot((q * qdec_ref[:, lanes]).astype(BF16), state.astype(BF16))
        kv = lax.dot_general((k * kdec_ref[:, lanes]).astype(BF16), vb, (((0,), (0,)), ((), ())),
                             preferred_element_type=F32)
        r_ref[hd] = state * cdec_ref[:, lanes] + kv
        mu = jnp.mean(ret, axis=-1, keepdims=True)
        cen = ret - mu
        var = jnp.mean(cen * cen, axis=-1, keepdims=True)
        normed = cen * lax.rsqrt(var + EPS) * rg_ref[:, lanes]
        merged_ref[:, lanes] = (normed * jax.nn.silu(g_ret)).astype(BF16)

    x_lru = proj[:, 4 * RET_WIDTH:4 * RET_WIDTH + LRU_WIDTH]
    g_lru = proj[:, 4 * RET_WIDTH + LRU_WIDTH:]
    xl_ref[8:8 + tc, :] = x_lru
    xc = cb_ref[...]
    for tap in range(CONV_WIDTH):
        start = 8 - (CONV_WIDTH - 1) + tap
        xc = xc + xl_ref[start:start + tc, :] * cw_ref[tap:tap + 1, :]
    xl_ref[0:8, :] = xl_ref[tc:tc + 8, :]
    lam = lam_ref[...]
    softplus_neg = jnp.maximum(-lam, 0.0) + jnp.log1p(jnp.exp(-jnp.abs(lam)))
    rows = lax.broadcasted_iota(jnp.int32, (tc // 8, 8, LRU_BLOCK_DIM), 1)
    for blk in range(LRU_BLOCKS):
        lanes = slice(blk * LRU_BLOCK_DIM, (blk + 1) * LRU_BLOCK_DIM)
        xb = xc[:, lanes]
        gates = _dot(xb.astype(BF16), wg_ref[blk])
        r = jax.nn.sigmoid(gates[:, :LRU_BLOCK_DIM] + ba_ref[:, lanes])
        i = jax.nn.sigmoid(gates[:, LRU_BLOCK_DIM:] + bi_ref[:, lanes])
        log_a = -LRU_C * r * softplus_neg[:, lanes]
        a = jnp.exp(log_a)
        mult = jnp.sqrt((1.0 + a * a) * jnp.tanh(-log_a))
        a = a.reshape(tc // 8, 8, LRU_BLOCK_DIM)
        b = (mult * i * xb).reshape(tc // 8, 8, LRU_BLOCK_DIM)
        for d in (1, 2, 4):
            keep = rows >= d
            a_prev = jnp.where(keep, pltpu.roll(a, d, 1), 1.0)
            b_prev = jnp.where(keep, pltpu.roll(b, d, 1), 0.0)
            b = a * b_prev + b
            a = a * a_prev
        carry = hc_ref[:, lanes]
        groups = []
        for j in range(tc // 8):
            hj = a[j] * carry + b[j]
            carry = hj[7:8, :]
            groups.append(hj)
        hc_ref[:, lanes] = carry
        hseq = jnp.concatenate(groups, axis=0)
        lru_lanes = slice(RET_WIDTH + blk * LRU_BLOCK_DIM, RET_WIDTH + (blk + 1) * LRU_BLOCK_DIM)
        merged_ref[:, lru_lanes] = (hseq * jax.nn.gelu(g_lru[:, lanes])).astype(BF16)

    out_ref[...] = x + _dot(merged_ref[...], wout_ref[...])


def _rope_tables(seq):
    half = HEAD_DIM // 2
    inv = ROPE_BASE ** (-jnp.arange(half, dtype=F32) / half)
    ang = jnp.arange(seq, dtype=F32)[:, None] * inv[None, :]
    cos = jnp.cos(ang)
    sin = jnp.sin(ang)
    return jnp.concatenate([cos, cos], axis=-1), jnp.concatenate([-sin, sin], axis=-1)


def _decay_tables(chunk):
    log_gamma = jnp.log1p(-jnp.power(2.0, -5.0 - jnp.arange(RET_HEADS, dtype=F32)))
    pos = jnp.arange(chunk, dtype=F32)
    diff = pos[:, None] - pos[None, :]
    decay = jnp.where(diff >= 0, jnp.exp(log_gamma[:, None, None] * jnp.maximum(diff, 0.0)), 0.0)
    k_decay = jnp.exp(log_gamma[:, None] * (chunk - 1.0 - pos)[None, :])
    q_decay = jnp.exp(log_gamma[:, None] * (pos + 1.0)[None, :])
    chunk_decay = jnp.exp(log_gamma * chunk)

    def lanes(t):
        return jnp.repeat(t.T, HEAD_DIM, axis=1)

    return decay, lanes(q_decay), lanes(k_decay), lanes(chunk_decay[:, None])


def _even_mixer(x, batch, seq, g_mix, w_in, w_out, ret_g, conv_w, conv_b, w_gates, b_a, b_i, lam):
    tc = MIX_ROWS
    tiles = seq // tc
    cos, sin = _rope_tables(seq)
    decay, q_decay, k_decay, chunk_decay = _decay_tables(tc)
    row_spec = pl.BlockSpec((tc, D_MODEL), lambda b, j: (b * tiles + j, 0))
    pos_spec = pl.BlockSpec((tc, HEAD_DIM), lambda b, j: (j, 0))
    return pl.pallas_call(
        _even_body,
        grid=(batch, tiles),
        in_specs=[row_spec, _const_spec((1, D_MODEL)), _const_spec((D_MODEL, IN_EVEN_WIDTH)),
                  _const_spec((D_MODEL, D_MODEL)), _const_spec((1, RET_WIDTH)),
                  _const_spec((CONV_WIDTH, LRU_WIDTH)), _const_spec((1, LRU_WIDTH)),
                  _const_spec((LRU_BLOCKS, LRU_BLOCK_DIM, 2 * LRU_BLOCK_DIM)),
                  _const_spec((1, LRU_WIDTH)), _const_spec((1, LRU_WIDTH)), _const_spec((1, LRU_WIDTH)),
                  pos_spec, pos_spec, _const_spec((RET_HEADS, tc, tc)),
                  _const_spec((tc, RET_WIDTH)), _const_spec((tc, RET_WIDTH)), _const_spec((1, RET_WIDTH))],
        out_specs=row_spec,
        out_shape=jax.ShapeDtypeStruct(x.shape, F32),
        scratch_shapes=[pltpu.VMEM((RET_HEADS, HEAD_DIM, HEAD_DIM), F32),
                        pltpu.VMEM((1, LRU_WIDTH), F32),
                        pltpu.VMEM((tc + 8, LRU_WIDTH), F32),
                        pltpu.VMEM((tc, D_MODEL), BF16)],
        compiler_params=pltpu.CompilerParams(dimension_semantics=("parallel", "arbitrary"),
                                             vmem_limit_bytes=VMEM_LIMIT_BYTES),
        name="even_mixer",
    )(x, g_mix, w_in, w_out, ret_g, conv_w, conv_b, w_gates, b_a, b_i, lam,
      cos, sin, decay, q_decay, k_decay, chunk_decay)


def _s5_discretize(lam_re, lam_im, log_dt):
    dt = jnp.exp(log_dt)
    mag = jnp.exp(lam_re * dt)
    lbar_re = mag * jnp.cos(lam_im * dt)
    lbar_im = mag * jnp.sin(lam_im * dt)
    den = lam_re * lam_re + lam_im * lam_im
    nr = lbar_re - 1.0
    ni = lbar_im
    f_re = (nr * lam_re + ni * lam_im) / den
    f_im = (ni * lam_re - nr * lam_im) / den
    return lbar_re, lbar_im, f_re, f_im


def _s5_ops_body(lre_row_ref, lim_row_ref, ldt_row_ref, lre_col_ref, lim_col_ref, ldt_col_ref,
                 bt_re_ref, bt_im_ref, ct_re_ref, ct_im_ref,
                 a_re_ref, a_im_ref, k_ref, mo_re_ref, mo_im_ref, l8_re_ref, l8_im_ref):
    lr, li, f_re, f_im = _s5_discretize(lre_row_ref[...], lim_row_ref[...], ldt_row_ref[...])
    bt_re = bt_re_ref[...]
    bt_im = bt_im_ref[...]
    bbar_re = f_re * bt_re - f_im * bt_im
    bbar_im = f_re * bt_im + f_im * bt_re
    ct_re = ct_re_ref[...]
    ct_im = ct_im_ref[...]
    pr = jnp.ones_like(lr)
    pi = jnp.zeros_like(li)
    for tau in range(S5_STEP):
        a_re = bbar_re * pr - bbar_im * pi
        a_im = bbar_re * pi + bbar_im * pr
        a_re_ref[:, tau] = a_re
        a_im_ref[:, tau] = a_im
        for gi in range(S5_OPS_GROUPS):
            k_ref[gi, tau] = (
                jnp.dot(a_re[gi], ct_re[gi], precision=lax.Precision.HIGHEST, preferred_element_type=F32)
                - jnp.dot(a_im[gi], ct_im[gi], precision=lax.Precision.HIGHEST, preferred_element_type=F32))
        pr, pi = pr * lr - pi * li, pr * li + pi * lr
    l8_re_ref[...] = pr
    l8_im_ref[...] = pi

    lr_c, li_c, _, _ = _s5_discretize(lre_col_ref[...], lim_col_ref[...], ldt_col_ref[...])
    qr = lr_c
    qi = li_c
    for t in range(S5_STEP):
        mo_re_ref[:, t] = ct_re * qr - ct_im * qi
        mo_im_ref[:, t] = -(ct_re * qi) - ct_im * qr
        qr, qi = qr * lr_c - qi * li_c, qr * li_c + qi * lr_c


def _s5_operators(lam_re, lam_im, log_dt, b_re, b_im, c_re, c_im):
    g, p, c = S5_GROUPS, S5_STATE, S5_GROUP
    ldt = jnp.broadcast_to(log_dt[:, None], (g, p))
    rows = [a.reshape(g, 1, p) for a in (lam_re, lam_im, ldt)]
    cols = [a.reshape(g, p, 1) for a in (lam_re, lam_im, ldt)]
    bt = [jnp.swapaxes(a, 1, 2) for a in (b_re, b_im)]
    ct = [jnp.swapaxes(a, 1, 2) for a in (c_re, c_im)]

    def spec(*shape):
        nd = len(shape)
        return pl.BlockSpec((S5_OPS_GROUPS,) + shape, lambda i: (i,) + (0,) * nd)

    a_re, a_im, kt, mo_re, mo_im, l8_re, l8_im = pl.pallas_call(
        _s5_ops_body,
        grid=(g // S5_OPS_GROUPS,),
        in_specs=[spec(1, p)] * 3 + [spec(p, 1)] * 3 + [spec(c, p)] * 2 + [spec(p, c)] * 2,
        out_specs=[spec(S5_STEP, c, p), spec(S5_STEP, c, p), spec(S5_STEP, c, c),
                   spec(S5_STEP, p, c), spec(S5_STEP, p, c), spec(1, p), spec(1, p)],
        out_shape=[jax.ShapeDtypeStruct((g, S5_STEP, c, p), F32), jax.ShapeDtypeStruct((g, S5_STEP, c, p), F32),
                   jax.ShapeDtypeStruct((g, S5_STEP, c, c), F32),
                   jax.ShapeDtypeStruct((g, S5_STEP, p, c), F32), jax.ShapeDtypeStruct((g, S5_STEP, p, c), F32),
                   jax.ShapeDtypeStruct((g, 1, p), F32), jax.ShapeDtypeStruct((g, 1, p), F32)],
        compiler_params=pltpu.CompilerParams(dimension_semantics=("parallel",)),
        name="s5_operators",
    )(*rows, *cols, *bt, *ct)

    n = S5_STEP * c
    h = g // 2
    a_re, a_im, kt, mo_re, mo_im = (v.astype(BF16) for v in (a_re, a_im, kt, mo_re, mo_im))

    def block_diag(first, second):
        z = jnp.zeros_like(first)
        return jnp.concatenate([jnp.concatenate([first, z], axis=2), jnp.concatenate([z, second], axis=2)], axis=1)

    in_re = a_re[:, ::-1].reshape(h, 2, n, p)
    in_im = a_im[:, ::-1].reshape(h, 2, n, p)
    m_in = jnp.concatenate([block_diag(in_re[:, 0], in_re[:, 1]), block_diag(in_im[:, 0], in_im[:, 1])], axis=2)
    zero_block = jnp.zeros((g, c, c), BF16)
    rows = [jnp.concatenate([zero_block] * s + [kt[:, tau] for tau in range(S5_STEP - s)], axis=-1)
            for s in range(S5_STEP)]
    intra = jnp.stack(rows, axis=1).reshape(h, 2, n, n)
    m_intra = block_diag(intra[:, 0], intra[:, 1])
    out_re = mo_re.transpose(0, 2, 1, 3).reshape(h, 2, p, n)
    out_im = mo_im.transpose(0, 2, 1, 3).reshape(h, 2, p, n)
    m_out = jnp.concatenate([block_diag(out_re[:, 0], out_re[:, 1]), block_diag(out_im[:, 0], out_im[:, 1])], axis=1)

    l8 = jnp.concatenate([l8_re.reshape(h, 1, 2 * p), l8_im.reshape(h, 1, 2 * p)], axis=-1)
    return m_in, m_intra, m_out, l8


def _s5_body(batch, u_ref, min_ref, mintra_ref, mout_ref, l8_ref, y_ref, s_ref, h_ref):
    rows = u_ref.shape[0]
    half = S5_PAIR // 2
    pairs = S5_LANES // S5_PAIR
    for pp in range(pairs):
        lanes = slice(pp * S5_PAIR, (pp + 1) * S5_PAIR)
        s_ref[:, lanes] = _dot(u_ref[:, lanes], min_ref[pp])

    l8 = [(l8_ref[pp, :, :half], l8_ref[pp, :, half:]) for pp in range(pairs)]

    def step(n, carry):
        r0 = pl.multiple_of(n * batch, batch)
        new = []
        for pp in range(pairs):
            st_re, st_im = carry[pp]
            re_lanes = slice(pp * S5_PAIR, pp * S5_PAIR + half)
            im_lanes = slice(pp * S5_PAIR + half, (pp + 1) * S5_PAIR)
            h_ref[pl.ds(r0, batch), re_lanes] = st_re.astype(BF16)
            h_ref[pl.ds(r0, batch), im_lanes] = st_im.astype(BF16)
            lr, li = l8[pp]
            new.append((lr * st_re - li * st_im + s_ref[pl.ds(r0, batch), re_lanes],
                        lr * st_im + li * st_re + s_ref[pl.ds(r0, batch), im_lanes]))
        return tuple(new)

    zero = jnp.zeros((batch, half), F32)
    lax.fori_loop(0, rows // batch, step, tuple((zero, zero) for _ in range(pairs)), unroll=S5_SCAN_UNROLL)

    for pp in range(pairs):
        lanes = slice(pp * S5_PAIR, (pp + 1) * S5_PAIR)
        y_ref[:, lanes] = _dot(u_ref[:, lanes], mintra_ref[pp]) + _dot(h_ref[:, lanes], mout_ref[pp])


def _s5_scan(u, batch, m_in, m_intra, m_out, l8):
    rows, width = u.shape
    pairs = S5_LANES // S5_PAIR
    lane_spec = pl.BlockSpec((rows, S5_LANES), lambda i: (0, i))
    op_spec = pl.BlockSpec((pairs, S5_PAIR, S5_PAIR), lambda i: (i, 0, 0))
    return pl.pallas_call(
        functools.partial(_s5_body, batch),
        grid=(width // S5_LANES,),
        in_specs=[lane_spec, op_spec, op_spec, op_spec, pl.BlockSpec((pairs, 1, S5_PAIR), lambda i: (i, 0, 0))],
        out_specs=lane_spec,
        out_shape=jax.ShapeDtypeStruct((rows, width), F32),
        scratch_shapes=[pltpu.VMEM((rows, S5_LANES), F32), pltpu.VMEM((rows, S5_LANES), BF16)],
        compiler_params=pltpu.CompilerParams(dimension_semantics=("parallel",),
                                             vmem_limit_bytes=VMEM_LIMIT_BYTES),
        name="s5_scan",
    )(u, m_in, m_intra, m_out, l8)


def _glu_ffn_body(x_ref, ys_ref, gmix_ref, d_ref, wab_ref, g_ref, w1_ref, w3_ref, w2_ref, g2_ref, out_ref, scr_ref):
    batch, steps = x_ref.shape[0], x_ref.shape[1]
    pitch = steps + PITCH_PAD
    lane_blocks = D_MODEL // 128
    step = pl.program_id(0)

    def relayout_next(lb):
        for c in range(steps // S5_STEP):
            groups = [ys_ref[c, :, (lb * 8 + g8) * 128:(lb * 8 + g8 + 1) * 128] for g8 in range(8)]
            for s, blk in enumerate(_block_transpose(groups)):
                scr_ref[lb, pl.ds(c * S5_STEP + s, batch, stride=pitch), :] = blk

    @pl.when(step == 0)
    def _():
        for lb in range(lane_blocks):
            relayout_next(lb)

    @pl.when(step > 0)
    def _():
        x = x_ref[...].reshape(-1, D_MODEL)
        ys = jnp.concatenate(
            [jnp.concatenate([scr_ref[lb, b * pitch:b * pitch + steps, :] for b in range(batch)], axis=0)
             for lb in range(lane_blocks)], axis=1)
        y = ys + d_ref[...] * _rms(x, gmix_ref[...])
        ab = _dot(jax.nn.gelu(y).astype(BF16), wab_ref[...])
        x = x + ab[:, :D_MODEL] * jax.nn.sigmoid(ab[:, D_MODEL:])
        xn = _rms(x, g_ref[...]).astype(BF16)
        acc = jnp.zeros(x.shape, F32)
        for ci, f in enumerate(range(0, D_FF, FFN_COLS)):
            h1 = _dot(xn, w1_ref[:, f:f + FFN_COLS])
            h3 = _dot(xn, w3_ref[:, f:f + FFN_COLS])
            if ci < lane_blocks:
                relayout_next(ci)
            gate = (jax.nn.silu(h1) * h3).astype(BF16)
            acc = acc + _dot(gate, w2_ref[f:f + FFN_COLS, :])
        out_ref[...] = _rms(x + 0.5 * acc, g2_ref[...]).reshape(out_ref.shape)


def _glu_ffn(x, ys, g_mix, d, w_ab, g, w1, w3, w2, g2, layer, half):
    batch, seq, _ = x.shape
    steps = GLU_ROWS // batch
    tiles = seq // steps
    row_spec = pl.BlockSpec((batch, steps, D_MODEL), lambda j: (0, jnp.maximum(j - 1, 0), 0))
    ys_spec = pl.BlockSpec((steps // S5_STEP, batch, S5_STEP * D_MODEL), lambda j: (jnp.minimum(j, tiles - 1), 0, 0))

    def weight_spec(rows, cols):
        return pl.BlockSpec((None, None, rows, cols), lambda j: (layer, half, 0, 0), pipeline_mode=pl.Buffered(1))

    return pl.pallas_call(
        _glu_ffn_body,
        grid=(tiles + 1,),
        in_specs=[row_spec, ys_spec, _const_spec((1, D_MODEL)), _const_spec((1, D_MODEL)),
                  _const_spec((D_MODEL, 2 * D_MODEL)), _const_spec((1, D_MODEL)),
                  weight_spec(D_MODEL, D_FF), weight_spec(D_MODEL, D_FF), weight_spec(D_FF, D_MODEL),
                  _const_spec((1, D_MODEL))],
        out_specs=row_spec,
        out_shape=jax.ShapeDtypeStruct(x.shape, F32),
        scratch_shapes=[pltpu.VMEM((D_MODEL // 128, batch * (steps + PITCH_PAD), 128), F32)],
        compiler_params=pltpu.CompilerParams(dimension_semantics=("arbitrary",),
                                             vmem_limit_bytes=VMEM_LIMIT_BYTES),
        name="glu_ffn_final",
    )(x, ys, g_mix, d, w_ab, g, w1, w3, w2, g2)


def kernel(x, ffn_norm_g, ffn_w1, ffn_w3, ffn_w2, mix_norm_g, w_in_even, w_out_even, ret_norm_g, conv_w, conv_b, lru_w_a, lru_b_a, lru_w_i, lru_b_i, lru_lambda, s5_lambda_re, s5_lambda_im, s5_log_dt, s5_b_re, s5_b_im, s5_c_re, s5_c_im, s5_d, glu_w_a, glu_w_b, final_norm_g):
    batch, seq, _ = x.shape
    t = batch * seq
    chunk_steps = FFN_ROWS // batch
    assert FFN_ROWS == GLU_ROWS and FFN_ROWS % batch == 0 and batch % 8 == 0
    assert seq % MIX_ROWS == 0 and t % PLAIN_FFN_ROWS == 0 and chunk_steps % S5_STEP == 0 and seq % chunk_steps == 0
    assert S5_GROUPS % S5_OPS_GROUPS == 0
    row = lambda v: v.reshape(1, -1).astype(F32)
    w1, w3, w2 = ffn_w1.astype(BF16), ffn_w3.astype(BF16), ffn_w2.astype(BF16)

    def ffn(xv, layer, half, g2, post):
        return _ffn(xv, row(ffn_norm_g[layer, half]), w1, w3, w2, row(g2), post, layer, half)

    (xf,) = ffn(x.reshape(t, D_MODEL), 0, 0, final_norm_g, "plain")
    w_gates = jnp.concatenate([lru_w_a[0], lru_w_i[0]], axis=-1).astype(BF16)
    xf = _even_mixer(xf, batch, seq, row(mix_norm_g[0]), w_in_even[0].astype(BF16), w_out_even[0].astype(BF16),
                     row(ret_norm_g[0]), conv_w[0].astype(F32), row(conv_b[0]), w_gates,
                     row(lru_b_a[0]), row(lru_b_i[0]), row(lru_lambda[0]))
    (xf,) = ffn(xf, 0, 1, final_norm_g, "plain")

    x3, u = ffn(xf.reshape(batch, seq, D_MODEL), 1, 0, mix_norm_g[1], "chunked")
    chunks = seq // S5_STEP
    m_in, m_intra, m_out, l8 = _s5_operators(s5_lambda_re[0], s5_lambda_im[0], s5_log_dt[0],
                                             s5_b_re[0], s5_b_im[0], s5_c_re[0], s5_c_im[0])
    ys = _s5_scan(u.reshape(chunks * batch, S5_STEP * D_MODEL), batch, m_in, m_intra, m_out, l8)
    w_ab = jnp.concatenate([glu_w_a[0], glu_w_b[0]], axis=-1).astype(BF16)
    return _glu_ffn(x3, ys.reshape(chunks, batch, S5_STEP * D_MODEL), row(mix_norm_g[1]), row(s5_d[0]), w_ab,
                    row(ffn_norm_g[1, 1]), w1, w3, w2, row(final_norm_g), 1, 1)
```
